```python
import jax, jax.numpy as jnp
from jax import lax
import numpy as np

D_MODEL = 1024
BATCH = 16
SEQ = 4096
DEPTH = 4

CHUNK = 64
N_MEM = 256
EPS = 1e-6
D_A = D_MODEL // 2
D_B = D_MODEL - D_A
CONV_A = 3
CONV_B = 31
M_HEADS = 8
M_HEAD_DIM = D_MODEL // M_HEADS
X_HEADS = 4
X_HEAD_DIM = D_MODEL // X_HEADS
D_FF = ((8 * D_MODEL + 3 * 256 - 1) // (3 * 256)) * 256
N_EVEN = (DEPTH + 1) // 2
N_ODD = DEPTH // 2
A_IN = 3 * D_A + 2 * D_B
M_IN = 4 * D_MODEL + 2 * M_HEADS

kernel_name = "hybrid_conv_mlstm_memxattn_trunk"


def rmsnorm(x, g):
    xf = x.astype(jnp.float32)
    y = xf * lax.rsqrt(jnp.mean(xf * xf, axis=-1, keepdims=True) + EPS)
    return (y * g.astype(jnp.float32)).astype(x.dtype)


def layernorm(x, g, b):
    xf = x.astype(jnp.float32)
    mu = jnp.mean(xf, axis=-1, keepdims=True)
    var = jnp.mean(jnp.square(xf - mu), axis=-1, keepdims=True)
    y = (xf - mu) * lax.rsqrt(var + EPS)
    return (y * g.astype(jnp.float32) + b.astype(jnp.float32)).astype(x.dtype)


def causal_dwconv(x, w):
    k = w.shape[0]
    xp = jnp.pad(x, ((0, 0), (k - 1, 0), (0, 0)))
    return lax.conv_general_dilated(
        xp, w[:, None, :].astype(x.dtype), window_strides=(1,), padding="VALID",
        dimension_numbers=("NWC", "WIO", "NWC"), feature_group_count=x.shape[-1])


def conv_mixer(h, w_in, conv_a, conv_b, conv_b_bias, ln_g, ln_b, w_out):
    u = h @ w_in
    gb, gc, xa, ua, ub = jnp.split(u, [D_A, 2 * D_A, 3 * D_A, 3 * D_A + D_B], axis=-1)
    y_a = gb * causal_dwconv(gc * xa, conv_a)
    glu = ua * jax.nn.sigmoid(ub)
    z = causal_dwconv(glu, conv_b) + conv_b_bias
    y_b = jax.nn.silu(layernorm(z, ln_g, ln_b))
    return jnp.concatenate([y_a, y_b], axis=-1) @ w_out


def mlstm_mixer(h, w_in, i_bias, f_bias, mh_norm_g, w_out):
    bsz, s, _ = h.shape
    f32 = jnp.float32
    nc = s // CHUNK
    u = h @ w_in
    q, k, v, o, i_pre, f_pre = jnp.split(
        u, [D_MODEL, 2 * D_MODEL, 3 * D_MODEL, 4 * D_MODEL, 4 * D_MODEL + M_HEADS], axis=-1)
    i_log = (i_pre + i_bias).astype(f32)
    f_log = jax.nn.log_sigmoid((f_pre + f_bias).astype(f32))

    def heads_to_chunks(t):
        return t.astype(f32).reshape(bsz, nc, CHUNK, M_HEADS, M_HEAD_DIM).transpose(1, 0, 3, 2, 4)

    def gates_to_chunks(t):
        return t.reshape(bsz, nc, CHUNK, M_HEADS).transpose(1, 0, 3, 2)

    qc = heads_to_chunks(q)
    kc = heads_to_chunks(k) * (M_HEAD_DIM ** -0.5)
    vc = heads_to_chunks(v)
    ic = gates_to_chunks(i_log)
    fc = gates_to_chunks(f_log)
    causal = jnp.tril(jnp.ones((CHUNK, CHUNK), dtype=bool))

    def step(carry, xs):
        c_st, n_st, m_st = carry
        qq, kk, vv, ig, lf = xs
        b = jnp.cumsum(lf, axis=-1)
        g = b[..., -1]
        dmat = jnp.where(causal, b[..., :, None] - b[..., None, :] + ig[..., None, :], -jnp.inf)
        m_inter = b + m_st[..., None]
        m_t = jnp.maximum(m_inter, jnp.max(dmat, axis=-1))
        w = jnp.exp(dmat - m_t[..., None]) * jnp.einsum("bhld,bhsd->bhls", qq, kk)
        inter = jnp.exp(m_inter - m_t)
        num = inter[..., None] * jnp.einsum("bhld,bhdv->bhlv", qq, c_st) \
            + jnp.einsum("bhls,bhsv->bhlv", w, vv)
        den = inter * jnp.einsum("bhld,bhd->bhl", qq, n_st) + jnp.sum(w, axis=-1)
        out = num / jnp.maximum(jnp.abs(den), jnp.exp(-m_t))[..., None]
        a = g[..., None] - b + ig
        m_new = jnp.maximum(g + m_st, jnp.max(a, axis=-1))
        wa = jnp.exp(a - m_new[..., None])
        decay = jnp.exp(g + m_st - m_new)
        c_new = decay[..., None, None] * c_st + jnp.einsum("bhs,bhsk,bhsv->bhkv", wa, kk, vv)
        n_new = decay[..., None] * n_st + jnp.einsum("bhs,bhsk->bhk", wa, kk)
        return (c_new, n_new, m_new), out

    init = (jnp.zeros((bsz, M_HEADS, M_HEAD_DIM, M_HEAD_DIM), f32),
            jnp.zeros((bsz, M_HEADS, M_HEAD_DIM), f32),
            jnp.zeros((bsz, M_HEADS), f32))
    _, hc = lax.scan(step, init, (qc, kc, vc, ic, fc))
    hh = hc.transpose(1, 0, 3, 2, 4).reshape(bsz, s, M_HEADS, M_HEAD_DIM)
    mu = jnp.mean(hh, axis=-1, keepdims=True)
    var = jnp.mean(jnp.square(hh - mu), axis=-1, keepdims=True)
    hn = (hh - mu) * lax.rsqrt(var + EPS) * mh_norm_g.astype(f32).reshape(M_HEADS, M_HEAD_DIM)
    y = hn.reshape(bsz, s, D_MODEL).astype(h.dtype) * jax.nn.sigmoid(o)
    return y @ w_out


def mem_attention(h, memn, w_q, w_kv, w_o):
    bsz, s, _ = h.shape
    q = (h @ w_q).reshape(bsz, s, X_HEADS, X_HEAD_DIM)
    k, v = jnp.split(memn @ w_kv, 2, axis=-1)
    k = k.reshape(bsz, -1, X_HEADS, X_HEAD_DIM)
    v = v.reshape(bsz, -1, X_HEADS, X_HEAD_DIM)
    sc = jnp.einsum("bshd,bmhd->bhsm", q, k).astype(jnp.float32) * (X_HEAD_DIM ** -0.5)
    p = jax.nn.softmax(sc, axis=-1).astype(h.dtype)
    o = jnp.einsum("bhsm,bmhd->bshd", p, v).reshape(bsz, s, D_MODEL)
    return o @ w_o


def swiglu(h, w_gu, w_down):
    gt, up = jnp.split(h @ w_gu, 2, axis=-1)
    return (jax.nn.silu(gt) * up) @ w_down


def setup_inputs(seed: int = 0) -> dict:
    key = jax.random.key(seed)
    ks = jax.random.split(key, 24)
    nrm = jax.random.normal
    f32 = jnp.float32

    def lin(k, shape, fan_in):
        return nrm(k, shape, f32) * (fan_in ** -0.5)

    def gain(k, shape):
        return 1.0 + 0.02 * nrm(k, shape, f32)

    f_bias = jnp.linspace(3.0, 6.0, M_HEADS, dtype=f32)[None, :] + 0.1 * nrm(ks[14], (N_ODD, M_HEADS), f32)
    return {
        "x": nrm(ks[0], (BATCH, SEQ, D_MODEL), f32),
        "mem": nrm(ks[1], (BATCH, N_MEM, D_MODEL), f32),
        "norm_g": gain(ks[2], (DEPTH, 6, D_MODEL)),
        "mem_norm_g": gain(ks[3], (DEPTH, D_MODEL)),
        "a_w_in": lin(ks[4], (N_EVEN, D_MODEL, A_IN), D_MODEL),
        "a_conv_a": lin(ks[5], (N_EVEN, CONV_A, D_A), CONV_A),
        "a_conv_b": lin(ks[6], (N_EVEN, CONV_B, D_B), CONV_B),
        "a_conv_b_bias": 0.01 * nrm(ks[7], (N_EVEN, D_B), f32),
        "a_ln_g": gain(ks[8], (N_EVEN, D_B)),
        "a_ln_b": 0.02 * nrm(ks[9], (N_EVEN, D_B), f32),
        "a_w_out": lin(ks[10], (N_EVEN, D_MODEL, D_MODEL), D_MODEL),
        "m_w_in": lin(ks[11], (N_ODD, D_MODEL, M_IN), D_MODEL),
        "m_i_bias": 0.1 * nrm(ks[12], (N_ODD, M_HEADS), f32),
        "m_f_bias": f_bias,
        "m_norm_g": gain(ks[13], (N_ODD, D_MODEL)),
        "m_w_out": lin(ks[15], (N_ODD, D_MODEL, D_MODEL), D_MODEL),
        "x_w_q": lin(ks[16], (DEPTH, D_MODEL, D_MODEL), D_MODEL),
        "x_w_kv": lin(ks[17], (DEPTH, D_MODEL, 2 * D_MODEL), D_MODEL),
        "x_w_o": lin(ks[18], (DEPTH, D_MODEL, D_MODEL), D_MODEL),
        "f_w_gu": lin(ks[19], (DEPTH, D_MODEL, 2 * D_FF), D_MODEL),
        "f_w_down": lin(ks[20], (DEPTH, D_FF, D_MODEL), D_FF),
    }


def reference(x, mem, norm_g, mem_norm_g, a_w_in, a_conv_a, a_conv_b, a_conv_b_bias,
              a_ln_g, a_ln_b, a_w_out, m_w_in, m_i_bias, m_f_bias, m_norm_g, m_w_out,
              x_w_q, x_w_kv, x_w_o, f_w_gu, f_w_down):
    for layer in range(DEPTH):
        g = norm_g[layer]
        h = rmsnorm(x, g[0])
        if layer % 2 == 0:
            e = layer // 2
            y = conv_mixer(h, a_w_in[e], a_conv_a[e], a_conv_b[e], a_conv_b_bias[e],
                           a_ln_g[e], a_ln_b[e], a_w_out[e])
        else:
            o = layer // 2
            y = mlstm_mixer(h, m_w_in[o], m_i_bias[o], m_f_bias[o], m_norm_g[o], m_w_out[o])
        x = x + rmsnorm(y, g[1])
        memn = rmsnorm(mem, mem_norm_g[layer])
        x = x + rmsnorm(mem_attention(rmsnorm(x, g[2]), memn, x_w_q[layer], x_w_kv[layer],
                                      x_w_o[layer]), g[3])
        x = x + rmsnorm(swiglu(rmsnorm(x, g[4]), f_w_gu[layer], f_w_down[layer]), g[5])
    return x
```

```python
import functools

import jax
import jax.numpy as jnp
from jax import lax
from jax.experimental import pallas as pl
from jax.experimental.pallas import tpu as pltpu

F32 = jnp.float32
BF16 = jnp.bfloat16

EPS = 1e-6
CONV_A = 3
CONV_B = 31
M_HEADS = 8
X_HEADS = 4
LANES = 128
CONV_A_PAD = 8
CONV_B_PAD = 32
CONV_ROWS = 16
MLSTM_CHUNK = 128
VMEM_LIMIT = 56 * 1024 * 1024


def _dot(a, b):
    return jnp.dot(a, b, preferred_element_type=F32)


def _dot_nt(a, b):
    return lax.dot_general(a, b, (((1,), (1,)), ((), ())), preferred_element_type=F32)


def _dot_tn(a, b):
    return lax.dot_general(a, b, (((0,), (0,)), ((), ())), preferred_element_type=F32)


def _rms(x, g):
    return x * lax.rsqrt(jnp.mean(x * x, axis=-1, keepdims=True) + EPS) * g


def _sigmoid(x):
    return 1.0 / (1.0 + jnp.exp(-x))


def _split3(x):
    hi = x.astype(BF16)
    r1 = x - hi.astype(F32)
    mid = r1.astype(BF16)
    lo = (r1 - mid.astype(F32)).astype(BF16)
    return hi, mid, lo


def _params(n_grid):
    return pltpu.CompilerParams(dimension_semantics=("arbitrary",) * n_grid,
                                vmem_limit_bytes=VMEM_LIMIT)


def _const_spec(shape):
    nd = len(shape)
    return pl.BlockSpec(shape, lambda *_: (0,) * nd, pipeline_mode=pl.Buffered(1))


def _ffn_body(x_ref, g_ref, wg_ref, wu_ref, wd_ref, o_ref):
    x = x_ref[...]
    h = _rms(x, g_ref[0:1, :]).astype(BF16)
    gt = _dot(h, wg_ref[...])
    up = _dot(h, wu_ref[...])
    a = (gt * _sigmoid(gt) * up).astype(BF16)
    y = _dot(a, wd_ref[...])
    o_ref[...] = x + _rms(y, g_ref[1:2, :])


def _ffn(x, g2, wg, wu, wd, tm):
    t, d = x.shape
    dff = wg.shape[1]
    return pl.pallas_call(
        _ffn_body,
        out_shape=jax.ShapeDtypeStruct((t, d), F32),
        grid=(t // tm,),
        in_specs=[pl.BlockSpec((tm, d), lambda i: (i, 0)),
                  _const_spec((2, d)), _const_spec((d, dff)), _const_spec((d, dff)),
                  _const_spec((dff, d))],
        out_specs=pl.BlockSpec((tm, d), lambda i: (i, 0)),
        compiler_params=_params(1),
        name="ffn",
    )(x, g2, wg, wu, wd)


def _kv_body(mem_ref, g_ref, wk_ref, wv_ref, k_ref, v_ref):
    memn = _rms(mem_ref[0], g_ref[...]).astype(BF16)
    k_ref[0] = _dot(memn, wk_ref[...]).astype(BF16)
    v_ref[0] = _dot(memn, wv_ref[...]).astype(BF16)


def _mem_kv(mem, g, wk, wv):
    b, m, d = mem.shape
    return pl.pallas_call(
        _kv_body,
        out_shape=(jax.ShapeDtypeStruct((b, m, d), BF16), jax.ShapeDtypeStruct((b, m, d), BF16)),
        grid=(b,),
        in_specs=[pl.BlockSpec((1, m, d), lambda i: (i, 0, 0)),
                  _const_spec((1, d)), _const_spec((d, d)), _const_spec((d, d))],
        out_specs=(pl.BlockSpec((1, m, d), lambda i: (i, 0, 0)),
                   pl.BlockSpec((1, m, d), lambda i: (i, 0, 0))),
        compiler_params=_params(1),
        name="mem_kv",
    )(mem, g, wk, wv)


def _xattn_body(x_ref, g_ref, wq_ref, k_ref, v_ref, wo_ref, o_ref):
    x = x_ref[...]
    d = x.shape[-1]
    dh = d // X_HEADS
    h = _rms(x, g_ref[0:1, :]).astype(BF16)
    q = (_dot(h, wq_ref[...]) * (dh ** -0.5)).astype(BF16)
    outs = []
    for hd in range(X_HEADS):
        sl = slice(hd * dh, (hd + 1) * dh)
        sc = _dot_nt(q[:, sl], k_ref[0, :, sl])
        e = jnp.exp(sc - jnp.max(sc, axis=-1, keepdims=True))
        p = (e / jnp.sum(e, axis=-1, keepdims=True)).astype(BF16)
        outs.append(_dot(p, v_ref[0, :, sl]))
    o = jnp.concatenate(outs, axis=-1).astype(BF16)
    y = _dot(o, wo_ref[...])
    o_ref[...] = x + _rms(y, g_ref[1:2, :])


def _xattn(x, g2, wq, k, v, wo, seq, tm):
    t, d = x.shape
    m = v.shape[1]
    per = seq // tm
    return pl.pallas_call(
        _xattn_body,
        out_shape=jax.ShapeDtypeStruct((t, d), F32),
        grid=(t // tm,),
        in_specs=[pl.BlockSpec((tm, d), lambda i: (i, 0)),
                  _const_spec((2, d)), _const_spec((d, d)),
                  pl.BlockSpec((1, m, d), lambda i: (i // per, 0, 0)),
                  pl.BlockSpec((1, m, d), lambda i: (i // per, 0, 0)),
                  _const_spec((d, d))],
        out_specs=pl.BlockSpec((tm, d), lambda i: (i, 0)),
        compiler_params=_params(1),
        name="xattn",
    )(x, g2, wq, k, v, wo)


def _conv_body(x_ref, g_ref, win_ref, wa_ref, wb_ref, vec_ref, wout_ref, o_ref,
               abuf, bbuf, ybuf, *, tm, da, db):
    @pl.when(pl.program_id(1) == 0)
    def _():
        abuf[0:CONV_A_PAD, :] = jnp.zeros((CONV_A_PAD, da), F32)
        bbuf[0:CONV_B_PAD, :] = jnp.zeros((CONV_B_PAD, db), F32)

    x = x_ref[...]
    h = _rms(x, g_ref[0:1, :]).astype(BF16)
    u = _dot(h, win_ref[...])
    gate_b = u[:, 0:da]
    abuf[CONV_A_PAD:CONV_A_PAD + tm, :] = u[:, da:2 * da] * u[:, 2 * da:3 * da]
    bbuf[CONV_B_PAD:CONV_B_PAD + tm, :] = u[:, 3 * da:3 * da + db] * _sigmoid(u[:, 3 * da + db:])

    bias, ln_g, ln_b = vec_ref[0:1, :], vec_ref[1:2, :], vec_ref[2:3, :]
    for r in range(0, tm, CONV_ROWS):
        acc = None
        for k in range(CONV_A):
            s = CONV_A_PAD - (CONV_A - 1) + k + r
            term = wa_ref[k:k + 1, :] * abuf[s:s + CONV_ROWS, :]
            acc = term if acc is None else acc + term
        ybuf[r:r + CONV_ROWS, 0:da] = (gate_b[r:r + CONV_ROWS, :] * acc).astype(BF16)
        acc = None
        for k in range(CONV_B):
            s = CONV_B_PAD - (CONV_B - 1) + k + r
            term = wb_ref[k:k + 1, :] * bbuf[s:s + CONV_ROWS, :]
            acc = term if acc is None else acc + term
        z = acc + bias
        mu = jnp.mean(z, axis=-1, keepdims=True)
        zc = z - mu
        var = jnp.mean(zc * zc, axis=-1, keepdims=True)
        zn = zc * lax.rsqrt(var + EPS) * ln_g + ln_b
        ybuf[r:r + CONV_ROWS, da:da + db] = (zn * _sigmoid(zn)).astype(BF16)

    abuf[0:CONV_A_PAD, :] = abuf[tm:tm + CONV_A_PAD, :]
    bbuf[0:CONV_B_PAD, :] = bbuf[tm:tm + CONV_B_PAD, :]

    y = _dot(ybuf[...], wout_ref[...])
    o_ref[...] = x + _rms(y, g_ref[1:2, :])


def _conv_mixer(x, g2, win, wa, wb, vec, wout, seq, tm):
    t, d = x.shape
    da, db = wa.shape[1], wb.shape[1]
    per = seq // tm
    body = functools.partial(_conv_body, tm=tm, da=da, db=db)
    return pl.pallas_call(
        body,
        out_shape=jax.ShapeDtypeStruct((t, d), F32),
        grid=(t // seq, per),
        in_specs=[pl.BlockSpec((tm, d), lambda b, j: (b * per + j, 0)),
                  _const_spec((2, d)), _const_spec(win.shape), _const_spec(wa.shape),
                  _const_spec(wb.shape), _const_spec(vec.shape), _const_spec(wout.shape)],
        out_specs=pl.BlockSpec((tm, d), lambda b, j: (b * per + j, 0)),
        scratch_shapes=[pltpu.VMEM((CONV_A_PAD + tm, da), F32),
                        pltpu.VMEM((CONV_B_PAD + tm, db), F32),
                        pltpu.VMEM((tm, da + db), BF16)],
        compiler_params=_params(2),
        name="conv_mixer",
    )(x, g2, win, wa, wb, vec, wout)


def _log_sigmoid(x):
    return jnp.minimum(x, 0.0) - jnp.log(1.0 + jnp.exp(-jnp.abs(x)))


def _mlstm_body(x_ref, g_ref, wqkvo_ref, wif_ref, wift_ref, bias_r_ref, bias_c_ref, tri_ref,
                mhg_ref, wout_ref, o_ref, cst, mst, hbuf, *, tm, d):
    dh = d // M_HEADS
    L = MLSTM_CHUNK

    @pl.when(pl.program_id(1) == 0)
    def _():
        cst[...] = jnp.zeros(cst.shape, F32)
        mst[...] = jnp.zeros(mst.shape, F32)

    x = x_ref[...]
    h = _rms(x, g_ref[0:1, :]).astype(BF16)
    q = _dot(h, wqkvo_ref[:, 0:d]).astype(BF16)
    k = (_dot(h, wqkvo_ref[:, d:2 * d]) * (dh ** -0.5)).astype(BF16)
    v = _dot(h, wqkvo_ref[:, 2 * d:3 * d]).astype(BF16)
    og = _dot(h, wqkvo_ref[:, 3 * d:4 * d])

    gc = _dot(h, wif_ref[...]) + bias_r_ref[...]
    gr = _dot_nt(wift_ref[...], h) + bias_c_ref[...]
    is_f_c = lax.broadcasted_iota(jnp.int32, gc.shape, 1) >= M_HEADS
    is_f_r = lax.broadcasted_iota(jnp.int32, gr.shape, 0) >= M_HEADS
    gc = jnp.where(is_f_c, _log_sigmoid(gc), gc)
    gr = jnp.where(is_f_r, _log_sigmoid(gr), gr)
    tri = tri_ref[...]
    cs_c = sum(_dot(tri, part) for part in _split3(gc))
    cs_r = sum(_dot_nt(part, tri) for part in _split3(gr))

    row = lax.broadcasted_iota(jnp.int32, (L, L), 0)
    col = lax.broadcasted_iota(jnp.int32, (L, L), 1)
    causal = col <= row
    ones_col = (lax.broadcasted_iota(jnp.int32, (L, dh), 1) == 0).astype(BF16)

    for hd in range(M_HEADS):
        hs = slice(hd * dh, (hd + 1) * dh)
        for c in range(tm // L):
            ts = slice(c * L, (c + 1) * L)
            qh, kh, vh = q[ts, hs], k[ts, hs], v[ts, hs]
            vaug = jnp.concatenate([vh, ones_col], axis=-1)
            b_c = cs_c[ts, M_HEADS + hd:M_HEADS + hd + 1]
            i_c = gc[ts, hd:hd + 1]
            b_r = cs_r[M_HEADS + hd:M_HEADS + hd + 1, ts]
            i_r = gr[hd:hd + 1, ts]
            m_st = mst[hd][0:1, 0:1]
            g_tot = b_c[L - 1:L, :]

            dmat = jnp.where(causal, b_c - b_r + i_r, -jnp.inf)
            m_inter = b_c + m_st
            m_t = jnp.maximum(m_inter, jnp.max(dmat, axis=-1, keepdims=True))
            w = (jnp.exp(dmat - m_t) * _dot_nt(qh, kh)).astype(BF16)
            inter = jnp.exp(m_inter - m_t)
            caug = cst[hd]
            tot = inter * _dot(qh, caug.astype(BF16)) + _dot(w, vaug)
            num = tot[:, 0:dh]
            den = tot[:, dh:dh + 1]
            out = num / jnp.maximum(jnp.abs(den), jnp.exp(-m_t))

            a = g_tot - b_c + i_c
            m_new = jnp.maximum(g_tot + m_st, jnp.max(a, axis=0, keepdims=True))
            wa = jnp.exp(a - m_new)
            decay = jnp.exp(g_tot + m_st - m_new)
            cst[hd] = decay * caug + _dot_tn(kh, (wa * vaug.astype(F32)).astype(BF16))
            mst[hd] = jnp.broadcast_to(m_new, mst.shape[1:])

            mu = jnp.mean(out, axis=-1, keepdims=True)
            oc = out - mu
            var = jnp.mean(oc * oc, axis=-1, keepdims=True)
            hn = oc * lax.rsqrt(var + EPS) * mhg_ref[:, hs]
            hbuf[ts, hs] = (hn * _sigmoid(og[ts, hs])).astype(BF16)

    y = _dot(hbuf[...], wout_ref[...])
    o_ref[...] = x + _rms(y, g_ref[1:2, :])


def _mlstm_mixer(x, g2, wqkvo, wif, wift, bias_r, bias_c, tri, mhg, wout, seq, tm):
    t, d = x.shape
    dh = d // M_HEADS
    per = seq // tm
    body = functools.partial(_mlstm_body, tm=tm, d=d)
    consts = (g2, wqkvo, wif, wift, bias_r, bias_c, tri, mhg, wout)
    return pl.pallas_call(
        body,
        out_shape=jax.ShapeDtypeStruct((t, d), F32),
        grid=(t // seq, per),
        in_specs=[pl.BlockSpec((tm, d), lambda b, j: (b * per + j, 0))]
                 + [_const_spec(c.shape) for c in consts],
        out_specs=pl.BlockSpec((tm, d), lambda b, j: (b * per + j, 0)),
        scratch_shapes=[pltpu.VMEM((M_HEADS, dh, 2 * dh), F32),
                        pltpu.VMEM((M_HEADS, 8, LANES), F32),
                        pltpu.VMEM((tm, d), BF16)],
        compiler_params=_params(2),
        name="mlstm_mixer",
    )(x, *consts)


def _tile(n, want):
    t = min(want, n)
    assert n % t == 0, (n, t)
    return t


def kernel(x, mem, norm_g, mem_norm_g, a_w_in, a_conv_a, a_conv_b, a_conv_b_bias, a_ln_g, a_ln_b, a_w_out, m_w_in, m_i_bias, m_f_bias, m_norm_g, m_w_out, x_w_q, x_w_kv, x_w_o, f_w_gu, f_w_down):
    bsz, seq, d = x.shape
    depth = norm_g.shape[0]
    dff = f_w_down.shape[1]
    assert seq % MLSTM_CHUNK == 0 and d % (M_HEADS * LANES) == 0
    tm_ffn = _tile(seq, 512)
    tm_att = _tile(seq, 512)
    tm_conv = _tile(seq, 512)
    tm_ml = _tile(seq, 256)

    ml_idx = jnp.arange(tm_ml)
    tri = ((ml_idx[None, :] <= ml_idx[:, None])
           & (ml_idx[None, :] // MLSTM_CHUNK == ml_idx[:, None] // MLSTM_CHUNK)).astype(BF16)

    xt = x.reshape(bsz * seq, d)
    for layer in range(depth):
        g = norm_g[layer]
        if layer % 2 == 0:
            e = layer // 2
            vec = jnp.stack([a_conv_b_bias[e], a_ln_g[e], a_ln_b[e]])
            xt = _conv_mixer(xt, g[0:2], a_w_in[e].astype(BF16), a_conv_a[e], a_conv_b[e], vec,
                             a_w_out[e].astype(BF16), seq, tm_conv)
        else:
            o = layer // 2
            w_in = m_w_in[o]
            wif = jnp.pad(w_in[:, 4 * d:], ((0, 0), (0, LANES - 2 * M_HEADS))).astype(BF16)
            bias = jnp.pad(jnp.concatenate([m_i_bias[o], m_f_bias[o]]), (0, LANES - 2 * M_HEADS))
            xt = _mlstm_mixer(xt, g[0:2], w_in[:, :4 * d].astype(BF16), wif, wif.T,
                              bias[None, :], bias[:, None], tri, m_norm_g[o][None, :],
                              m_w_out[o].astype(BF16), seq, tm_ml)
        w_kv = x_w_kv[layer].astype(BF16)
        km, vm = _mem_kv(mem, mem_norm_g[layer][None, :], w_kv[:, :d], w_kv[:, d:])
        xt = _xattn(xt, g[2:4], x_w_q[layer].astype(BF16), km, vm, x_w_o[layer].astype(BF16),
                    seq, tm_att)
        w_gu = f_w_gu[layer].astype(BF16)
        xt = _ffn(xt, g[4:6], w_gu[:, :dff], w_gu[:, dff:], f_w_down[layer].astype(BF16), tm_ffn)
    return xt.reshape(bsz, seq, d)
```

```python
import functools

import jax
import jax.numpy as jnp
from jax import lax
from jax.experimental import pallas as pl
from jax.experimental.pallas import tpu as pltpu

F32 = jnp.float32
BF16 = jnp.bfloat16

EPS = 1e-6
CONV_A = 3
CONV_B = 31
M_HEADS = 8
X_HEADS = 4
LANES = 128
SUBLANES = 8
STREAMS = 2
CONV_ROWS = 32
MLSTM_ONES = 16
TM_MIXER = 256
TM_DENSE = 512
VMEM_LIMIT = 56 * 1024 * 1024


def _dot(a, b):
    return jnp.dot(a, b, preferred_element_type=F32)


def _dot_nt(a, b):
    return lax.dot_general(a, b, (((1,), (1,)), ((), ())), preferred_element_type=F32)


def _dot_tn(a, b):
    return lax.dot_general(a, b, (((0,), (0,)), ((), ())), preferred_element_type=F32)


def _rms(x, g):
    return x * lax.rsqrt(jnp.mean(x * x, axis=-1, keepdims=True) + EPS) * g


def _sigmoid(x):
    return 1.0 / (1.0 + jnp.exp(-x))


def _log_sigmoid(x):
    return jnp.minimum(x, 0.0) - jnp.log(1.0 + jnp.exp(-jnp.abs(x)))


def _params(n_grid):
    return pltpu.CompilerParams(dimension_semantics=("arbitrary",) * n_grid,
                                vmem_limit_bytes=VMEM_LIMIT)


def _const_spec(shape):
    nd = len(shape)
    return pl.BlockSpec(shape, lambda *_: (0,) * nd, pipeline_mode=pl.Buffered(1))


def _ffn_body(x_ref, g_ref, wg_ref, wu_ref, wd_ref, o_ref):
    x = x_ref[...]
    h = _rms(x, g_ref[0:1, :]).astype(BF16)
    gt = _dot(h, wg_ref[...])
    up = _dot(h, wu_ref[...])
    a = (gt * _sigmoid(gt) * up).astype(BF16)
    y = _dot(a, wd_ref[...])
    o_ref[...] = x + _rms(y, g_ref[1:2, :])


def _ffn(x, g2, wg, wu, wd, tm):
    t, d = x.shape
    dff = wg.shape[1]
    return pl.pallas_call(
        _ffn_body,
        out_shape=jax.ShapeDtypeStruct((t, d), F32),
        grid=(t // tm,),
        in_specs=[pl.BlockSpec((tm, d), lambda i: (i, 0)),
                  _const_spec((2, d)), _const_spec((d, dff)), _const_spec((d, dff)),
                  _const_spec((dff, d))],
        out_specs=pl.BlockSpec((tm, d), lambda i: (i, 0)),
        compiler_params=_params(1),
        name="ffn",
    )(x, g2, wg, wu, wd)


def _kv_body(mem_ref, g_ref, wk_ref, wv_ref, k_ref, v_ref):
    memn = _rms(mem_ref[0], g_ref[...]).astype(BF16)
    k_ref[0] = _dot(memn, wk_ref[...]).astype(BF16)
    v_ref[0] = _dot(memn, wv_ref[...]).astype(BF16)


def _mem_kv(mem, g, wk, wv):
    b, m, d = mem.shape
    return pl.pallas_call(
        _kv_body,
        out_shape=(jax.ShapeDtypeStruct((b, m, d), BF16), jax.ShapeDtypeStruct((b, m, d), BF16)),
        grid=(b,),
        in_specs=[pl.BlockSpec((1, m, d), lambda i: (i, 0, 0)),
                  _const_spec((1, d)), _const_spec((d, d)), _const_spec((d, d))],
        out_specs=(pl.BlockSpec((1, m, d), lambda i: (i, 0, 0)),
                   pl.BlockSpec((1, m, d), lambda i: (i, 0, 0))),
        compiler_params=_params(1),
        name="mem_kv",
    )(mem, g, wk, wv)


def _xattn_body(x_ref, g_ref, wq_ref, k_ref, v_ref, wo_ref, o_ref):
    x = x_ref[...]
    d = x.shape[-1]
    dh = d // X_HEADS
    h = _rms(x, g_ref[0:1, :]).astype(BF16)
    q = (_dot(h, wq_ref[...]) * (dh ** -0.5)).astype(BF16)
    outs = []
    for hd in range(X_HEADS):
        sl = slice(hd * dh, (hd + 1) * dh)
        sc = _dot_nt(q[:, sl], k_ref[0, :, sl])
        e = jnp.exp(sc - jnp.max(sc, axis=-1, keepdims=True))
        p = (e / jnp.sum(e, axis=-1, keepdims=True)).astype(BF16)
        outs.append(_dot(p, v_ref[0, :, sl]))
    o = jnp.concatenate(outs, axis=-1).astype(BF16)
    y = _dot(o, wo_ref[...])
    o_ref[...] = x + _rms(y, g_ref[1:2, :])


def _xattn(x, g2, wq, k, v, wo, seq, tm):
    t, d = x.shape
    m = v.shape[1]
    per = seq // tm
    return pl.pallas_call(
        _xattn_body,
        out_shape=jax.ShapeDtypeStruct((t, d), F32),
        grid=(t // tm,),
        in_specs=[pl.BlockSpec((tm, d), lambda i: (i, 0)),
                  _const_spec((2, d)), _const_spec((d, d)),
                  pl.BlockSpec((1, m, d), lambda i: (i // per, 0, 0)),
                  pl.BlockSpec((1, m, d), lambda i: (i // per, 0, 0)),
                  _const_spec((d, d))],
        out_specs=pl.BlockSpec((tm, d), lambda i: (i, 0)),
        compiler_params=_params(1),
        name="xattn",
    )(x, g2, wq, k, v, wo)


def _pad_rows(taps):
    return -(-(taps - 1) // SUBLANES) * SUBLANES


def _tap_plan(taps):
    pad = _pad_rows(taps)
    plan = []
    for k in range(taps):
        off = pad - (taps - 1) + k
        plan.append((off % SUBLANES, off - off % SUBLANES))
    return plan


def _conv_body(x_ref, g_ref, win_ref, wa_ref, wb_ref, vec_ref, wout_ref, o_ref,
               abuf, bbuf, gbuf, ybuf, *, tm, da, db):
    pad_a, pad_b = _pad_rows(CONV_A), _pad_rows(CONV_B)
    plan_a, plan_b = _tap_plan(CONV_A), _tap_plan(CONV_B)
    shifts_a = sorted({s for s, _ in plan_a if s})
    shifts_b = sorted({s for s, _ in plan_b if s})
    sub = CONV_ROWS // SUBLANES

    @pl.when(pl.program_id(1) == 0)
    def _():
        for st in range(STREAMS):
            abuf[st, 0, 0:pad_a, :] = jnp.zeros((pad_a, da), F32)
            bbuf[st, 0, 0:pad_b, :] = jnp.zeros((pad_b, db), F32)

    bias, ln_g, ln_b = vec_ref[0:1, :], vec_ref[1:2, :], vec_ref[2:3, :]

    def stage_in(st):
        h = _rms(x_ref[st], g_ref[0:1, :]).astype(BF16)
        u = _dot(h, win_ref[...])
        gbuf[st] = u[:, 0:da]
        abuf[st, 0, pad_a:pad_a + tm, :] = u[:, da:2 * da] * u[:, 2 * da:3 * da]
        bbuf[st, 0, pad_b:pad_b + tm, :] = u[:, 3 * da:3 * da + db] * _sigmoid(u[:, 3 * da + db:])
        na = pad_a + tm - SUBLANES
        for i, s in enumerate(shifts_a):
            abuf[st, 1 + i, 0:na, :] = abuf[st, 0, s:s + na, :]
        nb = pad_b + tm - SUBLANES
        for i, s in enumerate(shifts_b):
            bbuf[st, 1 + i, 0:nb, :] = bbuf[st, 0, s:s + nb, :]

    def taps(w_ref, buf, st, plan, shifts, r):
        accs = [None] * sub
        for k, (s, off) in enumerate(plan):
            src = 0 if s == 0 else 1 + shifts.index(s)
            w = w_ref[k]
            for i in range(sub):
                lo = off + r + i * SUBLANES
                term = w * buf[st, src, lo:lo + SUBLANES, :]
                accs[i] = term if accs[i] is None else accs[i] + term
        return jnp.concatenate(accs, axis=0)

    def stage_conv(st):
        for r in range(0, tm, CONV_ROWS):
            ya = gbuf[st, r:r + CONV_ROWS, :] * taps(wa_ref, abuf, st, plan_a, shifts_a, r)
            ybuf[st, r:r + CONV_ROWS, 0:da] = ya.astype(BF16)
            z = taps(wb_ref, bbuf, st, plan_b, shifts_b, r) + bias
            mu = jnp.mean(z, axis=-1, keepdims=True)
            zc = z - mu
            var = jnp.mean(zc * zc, axis=-1, keepdims=True)
            zn = zc * lax.rsqrt(var + EPS) * ln_g + ln_b
            ybuf[st, r:r + CONV_ROWS, da:da + db] = (zn * _sigmoid(zn)).astype(BF16)
        abuf[st, 0, 0:pad_a, :] = abuf[st, 0, tm:tm + pad_a, :]
        bbuf[st, 0, 0:pad_b, :] = bbuf[st, 0, tm:tm + pad_b, :]

    def stage_proj(st):
        y = _dot(ybuf[st], wout_ref[...])
        o_ref[st] = x_ref[st] + _rms(y, g_ref[1:2, :])

    stage_in(0)
    stage_in(1)
    stage_conv(0)
    stage_proj(0)
    stage_conv(1)
    stage_proj(1)


def _conv_mixer(x, g2, win, wa, wb, vec, wout, seq, tm):
    ns, t, d = x.shape
    da, db = wa.shape[-1], wb.shape[-1]
    per = seq // tm
    pad_a, pad_b = _pad_rows(CONV_A), _pad_rows(CONV_B)
    n_sh_a = 1 + len({s for s, _ in _tap_plan(CONV_A) if s})
    n_sh_b = 1 + len({s for s, _ in _tap_plan(CONV_B) if s})
    body = functools.partial(_conv_body, tm=tm, da=da, db=db)
    return pl.pallas_call(
        body,
        out_shape=jax.ShapeDtypeStruct((ns, t, d), F32),
        grid=(t // seq, per),
        in_specs=[pl.BlockSpec((ns, tm, d), lambda b, j: (0, b * per + j, 0)),
                  _const_spec((2, d)), _const_spec(win.shape), _const_spec(wa.shape),
                  _const_spec(wb.shape), _const_spec(vec.shape), _const_spec(wout.shape)],
        out_specs=pl.BlockSpec((ns, tm, d), lambda b, j: (0, b * per + j, 0)),
        scratch_shapes=[pltpu.VMEM((ns, n_sh_a, pad_a + tm, da), F32),
                        pltpu.VMEM((ns, n_sh_b, pad_b + tm, db), F32),
                        pltpu.VMEM((ns, tm, da), F32),
                        pltpu.VMEM((ns, tm, da + db), BF16)],
        compiler_params=_params(2),
        name="conv_mixer",
    )(x, g2, win, wa, wb, vec, wout)


def _rep_rows(w):
    return jnp.broadcast_to(w[:, None, :], (w.shape[0], SUBLANES, w.shape[1]))


def _scan_lanes(x, op, fill):
    n = x.shape[-1]
    lane = lax.broadcasted_iota(jnp.int32, x.shape, x.ndim - 1)
    s = 1
    while s < n:
        x = op(x, jnp.where(lane >= s, pltpu.roll(x, s, x.ndim - 1), fill))
        s *= 2
    return x


def _mlstm_body(x_ref, g_ref, wt_ref, wk_ref, wift_ref, bias_c_ref, negt_ref, mhg_ref, wout_ref,
                o_ref, cst, mst, hbuf, sbuf, tbuf, *, tm, d):
    dh = d // M_HEADS
    L = tm
    heads = range(M_HEADS)

    @pl.when(pl.program_id(1) == 0)
    def _():
        cst[...] = jnp.zeros(cst.shape, F32)
        mst[...] = jnp.zeros(mst.shape, F32)

    ones_rows = jnp.ones((MLSTM_ONES, L), F32)
    env = [dict() for _ in range(STREAMS)]

    def hsl(hd):
        return slice(hd * dh, (hd + 1) * dh)

    def row(a, hd):
        return a[hd:hd + 1, :]

    def stage_in(st):
        e = env[st]
        h = _rms(x_ref[st], g_ref[0:1, :]).astype(BF16)
        e["qT"] = _dot_nt(wt_ref[0:d, :], h).astype(BF16)
        e["vT"] = _dot_nt(wt_ref[d:2 * d, :], h)
        e["ogT"] = _dot_nt(wt_ref[2 * d:3 * d, :], h)
        e["k"] = (_dot(h, wk_ref[...]) * (dh ** -0.5)).astype(BF16)
        g_r = _dot_nt(wift_ref[...], h) + bias_c_ref[...]
        i_r = g_r[0:M_HEADS, :]
        b_r = _scan_lanes(_log_sigmoid(g_r[M_HEADS:2 * M_HEADS, :]), jnp.add, 0.0)
        r_r = i_r - b_r
        m_old = mst[st][:, 0:1]
        mm_r = jnp.maximum(m_old, _scan_lanes(r_r, jnp.maximum, -jnp.inf))
        e["mm_r"] = mm_r
        e["inter_r"] = jnp.exp(m_old - mm_r)
        e["floor_r"] = jnp.exp(-(b_r + mm_r))
        g_col = b_r[:, L - 1:L]
        a_r = g_col - b_r + i_r
        m_new = jnp.maximum(g_col + m_old, jnp.max(a_r, axis=-1, keepdims=True))
        e["wa_r"] = jnp.exp(a_r - m_new)
        e["decay"] = jnp.exp(g_col + m_old - m_new)
        mst[st] = jnp.broadcast_to(m_new, mst.shape[1:])
        r_pad = jnp.concatenate([r_r, jnp.zeros((LANES - M_HEADS, L), F32)], axis=0)
        e["r_c"] = r_pad.T

    def stage_scores(st):
        e = env[st]
        for hd in heads:
            sbuf[st, hd] = _dot(e["k"][:, hsl(hd)], e["qT"][hsl(hd), :])

    def stage_weights(st):
        e = env[st]
        e["w"] = []
        for hd in heads:
            arg = (e["r_c"][:, hd:hd + 1] + negt_ref[...]) - row(e["mm_r"], hd)
            e["w"].append((jnp.exp(arg) * sbuf[st, hd]).astype(BF16))

    def stage_tot(st):
        e = env[st]
        for hd in heads:
            vaug = jnp.concatenate([e["vT"][hsl(hd), :], ones_rows], axis=0)
            caug = cst[st, hd]
            tbuf[st, hd] = (row(e["inter_r"], hd) * _dot(caug.astype(BF16), e["qT"][hsl(hd), :])
                            + _dot(vaug.astype(BF16), e["w"][hd]))
            vw = (vaug * row(e["wa_r"], hd)).astype(BF16)
            cst[st, hd] = e["decay"][hd:hd + 1, :] * caug + _dot(vw, e["k"][:, hsl(hd)])

    def stage_out(st):
        e = env[st]
        for hd in heads:
            tot = tbuf[st, hd]
            dd = jnp.maximum(jnp.abs(tot[dh:dh + 1, :]), row(e["floor_r"], hd))
            out = tot[0:dh, :] / dd
            mu = jnp.mean(out, axis=0, keepdims=True)
            oc = out - mu
            var = jnp.mean(oc * oc, axis=0, keepdims=True)
            hn = oc * lax.rsqrt(var + EPS) * mhg_ref[hsl(hd), :]
            hbuf[st, hsl(hd), :] = (hn * _sigmoid(e["ogT"][hsl(hd), :])).astype(BF16)

    def stage_proj(st):
        y = _dot_tn(hbuf[st], wout_ref[...])
        o_ref[st] = x_ref[st] + _rms(y, g_ref[1:2, :])

    stage_in(0)
    stage_scores(0)
    stage_in(1)
    stage_weights(0)
    stage_tot(0)
    stage_scores(1)
    stage_out(0)
    stage_weights(1)
    stage_proj(0)
    stage_tot(1)
    stage_out(1)
    stage_proj(1)


def _mlstm_mixer(x, g2, wt, wk, wift, bias_c, negt, mhg, wout, seq, tm):
    ns, t, d = x.shape
    dh = d // M_HEADS
    per = seq // tm
    body = functools.partial(_mlstm_body, tm=tm, d=d)
    consts = (g2, wt, wk, wift, bias_c, negt, mhg, wout)
    return pl.pallas_call(
        body,
        out_shape=jax.ShapeDtypeStruct((ns, t, d), F32),
        grid=(t // seq, per),
        in_specs=[pl.BlockSpec((ns, tm, d), lambda b, j: (0, b * per + j, 0))]
                 + [_const_spec(c.shape) for c in consts],
        out_specs=pl.BlockSpec((ns, tm, d), lambda b, j: (0, b * per + j, 0)),
        scratch_shapes=[pltpu.VMEM((ns, M_HEADS, dh + MLSTM_ONES, dh), F32),
                        pltpu.VMEM((ns, M_HEADS, LANES), F32),
                        pltpu.VMEM((ns, d, tm), BF16),
                        pltpu.VMEM((ns, M_HEADS, tm, tm), F32),
                        pltpu.VMEM((ns, M_HEADS, dh + MLSTM_ONES, tm), F32)],
        compiler_params=_params(2),
        name="mlstm_mixer",
    )(x, *consts)


def _mlstm_consts(w_in, i_bias, f_bias, mh_g, d, tm):
    wt = jnp.concatenate([w_in[:, 0:d], w_in[:, 2 * d:4 * d]], axis=1).T.astype(BF16)
    wk = w_in[:, d:2 * d].astype(BF16)
    wift = w_in[:, 4 * d:].T.astype(BF16)
    bias_c = jnp.concatenate([i_bias, f_bias])[:, None]
    idx = jnp.arange(tm)
    negt = jnp.where(idx[:, None] <= idx[None, :], 0.0, -jnp.inf).astype(F32)
    mhg = jnp.broadcast_to(mh_g[:, None], (d, tm))
    return wt, wk, wift, bias_c, negt, mhg


def kernel(x, mem, norm_g, mem_norm_g, a_w_in, a_conv_a, a_conv_b, a_conv_b_bias, a_ln_g, a_ln_b, a_w_out, m_w_in, m_i_bias, m_f_bias, m_norm_g, m_w_out, x_w_q, x_w_kv, x_w_o, f_w_gu, f_w_down):
    bsz, seq, d = x.shape
    depth = norm_g.shape[0]
    dff = f_w_down.shape[1]
    t = bsz * seq
    tm_mix = min(TM_MIXER, seq)
    tm_dense = min(TM_DENSE, seq)
    assert bsz % STREAMS == 0 and seq % tm_mix == 0 and seq % tm_dense == 0
    assert d % (M_HEADS * LANES) == 0 and tm_mix % CONV_ROWS == 0

    xt = x.reshape(t, d)
    for layer in range(depth):
        g = norm_g[layer]
        xs = xt.reshape(STREAMS, t // STREAMS, d)
        if layer % 2 == 0:
            e = layer // 2
            vec = jnp.stack([a_conv_b_bias[e], a_ln_g[e], a_ln_b[e]])
            xs = _conv_mixer(xs, g[0:2], a_w_in[e].astype(BF16), _rep_rows(a_conv_a[e]),
                             _rep_rows(a_conv_b[e]), vec, a_w_out[e].astype(BF16), seq, tm_mix)
        else:
            o = layer // 2
            consts = _mlstm_consts(m_w_in[o], m_i_bias[o], m_f_bias[o], m_norm_g[o], d, tm_mix)
            xs = _mlstm_mixer(xs, g[0:2], *consts, m_w_out[o].astype(BF16), seq, tm_mix)
        xt = xs.reshape(t, d)
        w_kv = x_w_kv[layer].astype(BF16)
        km, vm = _mem_kv(mem, mem_norm_g[layer][None, :], w_kv[:, :d], w_kv[:, d:])
        xt = _xattn(xt, g[2:4], x_w_q[layer].astype(BF16), km, vm, x_w_o[layer].astype(BF16),
                    seq, tm_dense)
        w_gu = f_w_gu[layer].astype(BF16)
        xt = _ffn(xt, g[4:6], w_gu[:, :dff], w_gu[:, dff:], f_w_down[layer].astype(BF16), tm_dense)
    return xt.reshape(bsz, seq, d)
```

```python
import functools

import jax
import jax.numpy as jnp
from jax import lax
from jax.experimental import pallas as pl
from jax.experimental.pallas import tpu as pltpu

F32 = jnp.float32
BF16 = jnp.bfloat16

EPS = 1e-6
CONV_A = 3
CONV_B = 31
M_HEADS = 8
X_HEADS = 4
LANES = 128
SUBLANES = 8
CONV_STREAMS = 2
MLSTM_STREAMS = 4
ATTN_PARTS = 4
FFN_PARTS = 4
CONV_ROWS = 16
MLSTM_ONES = 16
TM_MIXER = 256
MLSTM_LAG = 2
TM_ATTN = 2048
TM_FFN = 1024
VMEM_LIMIT = 56 * 1024 * 1024


def _dot(a, b):
    return jnp.dot(a, b, preferred_element_type=F32)


def _dot_nt(a, b):
    return lax.dot_general(a, b, (((1,), (1,)), ((), ())), preferred_element_type=F32)


def _dot_tn(a, b):
    return lax.dot_general(a, b, (((0,), (0,)), ((), ())), preferred_element_type=F32)


def _rms(x, g):
    return x * lax.rsqrt(jnp.mean(x * x, axis=-1, keepdims=True) + EPS) * g


def _sigmoid(x):
    return 1.0 / (1.0 + jnp.exp(-x))


def _log_sigmoid(x):
    return jnp.minimum(x, 0.0) - jnp.log(1.0 + jnp.exp(-jnp.abs(x)))


def _params(n_grid):
    return pltpu.CompilerParams(dimension_semantics=("arbitrary",) * n_grid,
                                vmem_limit_bytes=VMEM_LIMIT)


def _const_spec(shape):
    nd = len(shape)
    return pl.BlockSpec(shape, lambda *_: (0,) * nd, pipeline_mode=pl.Buffered(1))


def _ffn_body(x_ref, g_ref, wgu_ref, wd_ref, o_ref):
    dff = wd_ref.shape[0]
    tm = x_ref.shape[0] // FFN_PARTS
    rows = [slice(p * tm, (p + 1) * tm) for p in range(FFN_PARTS)]
    hs = [_rms(x_ref[r, :], g_ref[0:1, :]).astype(BF16) for r in rows]
    acts = []
    for h in hs:
        gt = _dot(h, wgu_ref[:, 0:dff])
        up = _dot(h, wgu_ref[:, dff:2 * dff])
        acts.append((gt * _sigmoid(gt) * up).astype(BF16))
    for r, a in zip(rows, acts):
        y = _dot(a, wd_ref[...])
        o_ref[r, :] = x_ref[r, :] + _rms(y, g_ref[1:2, :])


def _ffn(x, g2, wgu, wd, tm):
    t, d = x.shape
    return pl.pallas_call(
        _ffn_body,
        out_shape=jax.ShapeDtypeStruct((t, d), F32),
        grid=(t // tm,),
        in_specs=[pl.BlockSpec((tm, d), lambda i: (i, 0)),
                  _const_spec((2, d)), _const_spec(wgu.shape), _const_spec(wd.shape)],
        out_specs=pl.BlockSpec((tm, d), lambda i: (i, 0)),
        compiler_params=_params(1),
        name="ffn",
    )(x, g2, wgu, wd)


def _kv_body(mem_ref, g_ref, wkt_ref, wv_ref, kt_ref, v_ref):
    memn = _rms(mem_ref[0], g_ref[...]).astype(BF16)
    kt_ref[0] = _dot_nt(wkt_ref[...], memn).astype(BF16)
    v_ref[0] = _dot(memn, wv_ref[...]).astype(BF16)


def _mem_kv(mem, g, wkt, wv):
    b, m, d = mem.shape
    return pl.pallas_call(
        _kv_body,
        out_shape=(jax.ShapeDtypeStruct((b, d, m), BF16), jax.ShapeDtypeStruct((b, m, d), BF16)),
        grid=(b,),
        in_specs=[pl.BlockSpec((1, m, d), lambda i: (i, 0, 0)),
                  _const_spec((1, d)), _const_spec((d, d)), _const_spec((d, d))],
        out_specs=(pl.BlockSpec((1, d, m), lambda i: (i, 0, 0)),
                   pl.BlockSpec((1, m, d), lambda i: (i, 0, 0))),
        compiler_params=_params(1),
        name="mem_kv",
    )(mem, g, wkt, wv)


def _xattn_body(x_ref, g_ref, wq_ref, kt_ref, v_ref, wo_ref, o_ref, obuf):
    tm = x_ref.shape[0] // ATTN_PARTS
    d = x_ref.shape[-1]
    dh = d // X_HEADS
    qs = [None] * ATTN_PARTS

    def rows(p):
        return slice(p * tm, (p + 1) * tm)

    def stage_q(p):
        h = _rms(x_ref[rows(p), :], g_ref[0:1, :]).astype(BF16)
        qs[p] = (_dot(h, wq_ref[...]) * (dh ** -0.5)).astype(BF16)

    def stage_att(p):
        for hd in range(X_HEADS):
            sl = slice(hd * dh, (hd + 1) * dh)
            sc = _dot(qs[p][:, sl], kt_ref[0, sl, :])
            e = jnp.exp(sc - jnp.max(sc, axis=-1, keepdims=True))
            pr = (e / jnp.sum(e, axis=-1, keepdims=True)).astype(BF16)
            obuf[rows(p), sl] = _dot(pr, v_ref[0, :, sl]).astype(BF16)

    def stage_o(p):
        y = _dot(obuf[rows(p), :], wo_ref[...])
        o_ref[rows(p), :] = x_ref[rows(p), :] + _rms(y, g_ref[1:2, :])

    stage_q(0)
    for p in range(ATTN_PARTS):
        if p + 1 < ATTN_PARTS:
            stage_q(p + 1)
        stage_att(p)
        stage_o(p)


def _xattn(x, g2, wq, kt, v, wo, seq, tm):
    t, d = x.shape
    m = v.shape[1]
    per = seq // tm
    return pl.pallas_call(
        _xattn_body,
        out_shape=jax.ShapeDtypeStruct((t, d), F32),
        grid=(t // tm,),
        in_specs=[pl.BlockSpec((tm, d), lambda i: (i, 0)),
                  _const_spec((2, d)), _const_spec((d, d)),
                  pl.BlockSpec((1, d, m), lambda i: (i // per, 0, 0)),
                  pl.BlockSpec((1, m, d), lambda i: (i // per, 0, 0)),
                  _const_spec((d, d))],
        out_specs=pl.BlockSpec((tm, d), lambda i: (i, 0)),
        scratch_shapes=[pltpu.VMEM((tm, d), BF16)],
        compiler_params=_params(1),
        name="xattn",
    )(x, g2, wq, kt, v, wo)


def _pad_rows(taps):
    return -(-(taps - 1) // SUBLANES) * SUBLANES


def _tap_plan(taps):
    pad = _pad_rows(taps)
    plan = []
    for k in range(taps):
        off = pad - (taps - 1) + k
        plan.append((off % SUBLANES, off - off % SUBLANES))
    return plan


def _conv_body(x_ref, g_ref, win_ref, wa_ref, wb_ref, vec_ref, wout_ref, o_ref,
               abuf, bbuf, gbuf, ybuf, *, tm, da, db):
    pad_a, pad_b = _pad_rows(CONV_A), _pad_rows(CONV_B)
    plan_a, plan_b = _tap_plan(CONV_A), _tap_plan(CONV_B)
    shifts_a = sorted({s for s, _ in plan_a if s})
    shifts_b = sorted({s for s, _ in plan_b if s})
    sub = CONV_ROWS // SUBLANES
    streams = x_ref.shape[0]

    @pl.when(pl.program_id(1) == 0)
    def _():
        for st in range(streams):
            abuf[st, 0, 0:pad_a, :] = jnp.zeros((pad_a, da), F32)
            bbuf[st, 0, 0:pad_b, :] = jnp.zeros((pad_b, db), F32)

    bias, ln_g, ln_b = vec_ref[0:1, :], vec_ref[1:2, :], vec_ref[2:3, :]

    def stage_in(st):
        h = _rms(x_ref[st], g_ref[0:1, :]).astype(BF16)
        u = _dot(h, win_ref[...])
        gbuf[st] = u[:, 0:da]
        abuf[st, 0, pad_a:pad_a + tm, :] = u[:, da:2 * da] * u[:, 2 * da:3 * da]
        bbuf[st, 0, pad_b:pad_b + tm, :] = u[:, 3 * da:3 * da + db] * _sigmoid(u[:, 3 * da + db:])
        na = pad_a + tm - SUBLANES
        for i, s in enumerate(shifts_a):
            abuf[st, 1 + i, 0:na, :] = abuf[st, 0, s:s + na, :]
        nb = pad_b + tm - SUBLANES
        for i, s in enumerate(shifts_b):
            bbuf[st, 1 + i, 0:nb, :] = bbuf[st, 0, s:s + nb, :]

    def taps(w_ref, buf, st, plan, shifts, r):
        accs = [None] * sub
        for k, (s, off) in enumerate(plan):
            src = 0 if s == 0 else 1 + shifts.index(s)
            w = w_ref[k]
            for i in range(sub):
                lo = off + r + i * SUBLANES
                term = w * buf[st, src, lo:lo + SUBLANES, :]
                accs[i] = term if accs[i] is None else accs[i] + term
        return jnp.concatenate(accs, axis=0)

    def stage_conv(st):
        for r in range(0, tm, CONV_ROWS):
            ya = gbuf[st, r:r + CONV_ROWS, :] * taps(wa_ref, abuf, st, plan_a, shifts_a, r)
            ybuf[st, r:r + CONV_ROWS, 0:da] = ya.astype(BF16)
            z = taps(wb_ref, bbuf, st, plan_b, shifts_b, r) + bias
            mu = jnp.mean(z, axis=-1, keepdims=True)
            zc = z - mu
            var = jnp.mean(zc * zc, axis=-1, keepdims=True)
            zn = zc * lax.rsqrt(var + EPS) * ln_g + ln_b
            ybuf[st, r:r + CONV_ROWS, da:da + db] = (zn * _sigmoid(zn)).astype(BF16)
        abuf[st, 0, 0:pad_a, :] = abuf[st, 0, tm:tm + pad_a, :]
        bbuf[st, 0, 0:pad_b, :] = bbuf[st, 0, tm:tm + pad_b, :]

    def stage_proj(st):
        y = _dot(ybuf[st], wout_ref[...])
        o_ref[st] = x_ref[st] + _rms(y, g_ref[1:2, :])

    for st in range(streams):
        stage_in(st)
    for st in range(streams):
        stage_conv(st)
        stage_proj(st)


def _conv_mixer(x, g2, win, wa, wb, vec, wout, seq, tm):
    ns, t, d = x.shape
    da, db = wa.shape[-1], wb.shape[-1]
    per = seq // tm
    pad_a, pad_b = _pad_rows(CONV_A), _pad_rows(CONV_B)
    n_sh_a = 1 + len({s for s, _ in _tap_plan(CONV_A) if s})
    n_sh_b = 1 + len({s for s, _ in _tap_plan(CONV_B) if s})
    body = functools.partial(_conv_body, tm=tm, da=da, db=db)
    return pl.pallas_call(
        body,
        out_shape=jax.ShapeDtypeStruct((ns, t, d), F32),
        grid=(t // seq, per),
        in_specs=[pl.BlockSpec((ns, tm, d), lambda b, j: (0, b * per + j, 0)),
                  _const_spec((2, d)), _const_spec(win.shape), _const_spec(wa.shape),
                  _const_spec(wb.shape), _const_spec(vec.shape), _const_spec(wout.shape)],
        out_specs=pl.BlockSpec((ns, tm, d), lambda b, j: (0, b * per + j, 0)),
        scratch_shapes=[pltpu.VMEM((ns, n_sh_a, pad_a + tm, da), F32),
                        pltpu.VMEM((ns, n_sh_b, pad_b + tm, db), F32),
                        pltpu.VMEM((ns, tm, da), F32),
                        pltpu.VMEM((ns, tm, da + db), BF16)],
        compiler_params=_params(2),
        name="conv_mixer",
    )(x, g2, win, wa, wb, vec, wout)


def _rep_rows(w):
    return jnp.broadcast_to(w[:, None, :], (w.shape[0], SUBLANES, w.shape[1]))


def _scan_lanes(x, op, fill):
    n = x.shape[-1]
    lane = lax.broadcasted_iota(jnp.int32, x.shape, x.ndim - 1)
    s = 1
    while s < n:
        x = op(x, jnp.where(lane >= s, pltpu.roll(x, s, x.ndim - 1), fill))
        s *= 2
    return x


def _mlstm_body(x_ref, g_ref, wt_ref, wk_ref, wift_ref, bias_c_ref, negt_ref, mhg_ref, wout_ref,
                o_ref, cst, mst, hbuf, sbuf, tbuf, *, tm, d):
    dh = d // M_HEADS
    L = tm
    heads = range(M_HEADS)

    @pl.when(pl.program_id(1) == 0)
    def _():
        cst[...] = jnp.zeros(cst.shape, F32)
        mst[...] = jnp.zeros(mst.shape, F32)

    ones_rows = jnp.ones((MLSTM_ONES, L), F32)
    streams = x_ref.shape[0]
    env = [dict() for _ in range(streams)]

    def hsl(hd):
        return slice(hd * dh, (hd + 1) * dh)

    def row(a, hd):
        return a[hd:hd + 1, :]

    def stage_in(st):
        e = env[st]
        h = _rms(x_ref[st], g_ref[0:1, :]).astype(BF16)
        e["qT"] = _dot_nt(wt_ref[0:d, :], h).astype(BF16)
        e["vT"] = _dot_nt(wt_ref[d:2 * d, :], h)
        e["ogT"] = _dot_nt(wt_ref[2 * d:3 * d, :], h)
        e["k"] = (_dot(h, wk_ref[...]) * (dh ** -0.5)).astype(BF16)
        g_r = _dot_nt(wift_ref[...], h) + bias_c_ref[...]
        i_r = g_r[0:M_HEADS, :]
        b_r = _scan_lanes(_log_sigmoid(g_r[M_HEADS:2 * M_HEADS, :]), jnp.add, 0.0)
        r_r = i_r - b_r
        m_old = mst[st][:, 0:1]
        mm_r = jnp.maximum(m_old, _scan_lanes(r_r, jnp.maximum, -jnp.inf))
        e["mm_r"] = mm_r
        e["inter_r"] = jnp.exp(m_old - mm_r)
        e["floor_r"] = jnp.exp(-(b_r + mm_r))
        g_col = b_r[:, L - 1:L]
        a_r = g_col - b_r + i_r
        m_new = jnp.maximum(g_col + m_old, jnp.max(a_r, axis=-1, keepdims=True))
        e["wa_r"] = jnp.exp(a_r - m_new)
        e["decay"] = jnp.exp(g_col + m_old - m_new)
        mst[st] = jnp.broadcast_to(m_new, mst.shape[1:])
        r_pad = jnp.concatenate([r_r, jnp.zeros((LANES - M_HEADS, L), F32)], axis=0)
        e["r_c"] = r_pad.T

    def stage_scores(st):
        e = env[st]
        for hd in heads:
            sbuf[st, hd] = _dot(e["k"][:, hsl(hd)], e["qT"][hsl(hd), :])

    def stage_weights(st):
        e = env[st]
        e["w"] = []
        for hd in heads:
            arg = (e["r_c"][:, hd:hd + 1] + negt_ref[...]) - row(e["mm_r"], hd)
            e["w"].append((jnp.exp(arg) * sbuf[st, hd]).astype(BF16))

    def stage_tot(st):
        e = env[st]
        for hd in heads:
            vaug = jnp.concatenate([e["vT"][hsl(hd), :], ones_rows], axis=0)
            caug = cst[st, hd]
            tbuf[st, hd] = (row(e["inter_r"], hd) * _dot(caug.astype(BF16), e["qT"][hsl(hd), :])
                            + _dot(vaug.astype(BF16), e["w"][hd]))
            vw = (vaug * row(e["wa_r"], hd)).astype(BF16)
            cst[st, hd] = e["decay"][hd:hd + 1, :] * caug + _dot(vw, e["k"][:, hsl(hd)])

    def stage_out(st):
        e = env[st]
        for hd in heads:
            tot = tbuf[st, hd]
            dd = jnp.maximum(jnp.abs(tot[dh:dh + 1, :]), row(e["floor_r"], hd))
            out = tot[0:dh, :] / dd
            mu = jnp.mean(out, axis=0, keepdims=True)
            oc = out - mu
            var = jnp.mean(oc * oc, axis=0, keepdims=True)
            hn = oc * lax.rsqrt(var + EPS) * mhg_ref[hsl(hd), :]
            hbuf[st, hsl(hd), :] = (hn * _sigmoid(e["ogT"][hsl(hd), :])).astype(BF16)

    def stage_proj(st):
        y = _dot_tn(hbuf[st], wout_ref[...])
        o_ref[st] = x_ref[st] + _rms(y, g_ref[1:2, :])

    stages = (stage_in, stage_scores, stage_weights, stage_tot, stage_out, stage_proj)
    for slot in range(MLSTM_LAG * (streams - 1) + len(stages)):
        for st in reversed(range(streams)):
            k = slot - MLSTM_LAG * st
            if 0 <= k < len(stages):
                stages[k](st)


def _mlstm_mixer(x, g2, wt, wk, wift, bias_c, negt, mhg, wout, seq, tm):
    ns, t, d = x.shape
    dh = d // M_HEADS
    per = seq // tm
    body = functools.partial(_mlstm_body, tm=tm, d=d)
    consts = (g2, wt, wk, wift, bias_c, negt, mhg, wout)
    return pl.pallas_call(
        body,
        out_shape=jax.ShapeDtypeStruct((ns, t, d), F32),
        grid=(t // seq, per),
        in_specs=[pl.BlockSpec((ns, tm, d), lambda b, j: (0, b * per + j, 0))]
                 + [_const_spec(c.shape) for c in consts],
        out_specs=pl.BlockSpec((ns, tm, d), lambda b, j: (0, b * per + j, 0)),
        scratch_shapes=[pltpu.VMEM((ns, M_HEADS, dh + MLSTM_ONES, dh), F32),
                        pltpu.VMEM((ns, M_HEADS, LANES), F32),
                        pltpu.VMEM((ns, d, tm), BF16),
                        pltpu.VMEM((ns, M_HEADS, tm, tm), F32),
                        pltpu.VMEM((ns, M_HEADS, dh + MLSTM_ONES, tm), F32)],
        compiler_params=_params(2),
        name="mlstm_mixer",
    )(x, *consts)


def _mlstm_consts(w_in, i_bias, f_bias, mh_g, d, tm):
    wt = jnp.concatenate([w_in[:, 0:d], w_in[:, 2 * d:4 * d]], axis=1).T.astype(BF16)
    wk = w_in[:, d:2 * d].astype(BF16)
    wift = w_in[:, 4 * d:].T.astype(BF16)
    bias_c = jnp.concatenate([i_bias, f_bias])[:, None]
    idx = jnp.arange(tm)
    negt = jnp.where(idx[:, None] <= idx[None, :], 0.0, -jnp.inf).astype(F32)
    mhg = jnp.broadcast_to(mh_g[:, None], (d, tm))
    return wt, wk, wift, bias_c, negt, mhg


def kernel(x, mem, norm_g, mem_norm_g, a_w_in, a_conv_a, a_conv_b, a_conv_b_bias, a_ln_g, a_ln_b, a_w_out, m_w_in, m_i_bias, m_f_bias, m_norm_g, m_w_out, x_w_q, x_w_kv, x_w_o, f_w_gu, f_w_down):
    bsz, seq, d = x.shape
    depth = norm_g.shape[0]
    t = bsz * seq
    tm_mix = min(TM_MIXER, seq)
    tm_att = min(TM_ATTN, seq)
    tm_ffn = min(TM_FFN, seq)
    assert bsz % CONV_STREAMS == 0 and bsz % MLSTM_STREAMS == 0
    assert seq % tm_mix == 0 and seq % tm_att == 0 and seq % tm_ffn == 0
    assert tm_att % ATTN_PARTS == 0 and tm_ffn % FFN_PARTS == 0
    assert d % (M_HEADS * LANES) == 0 and tm_mix % CONV_ROWS == 0

    xt = x.reshape(t, d)
    for layer in range(depth):
        g = norm_g[layer]
        if layer % 2 == 0:
            e = layer // 2
            xs = xt.reshape(CONV_STREAMS, t // CONV_STREAMS, d)
            vec = jnp.stack([a_conv_b_bias[e], a_ln_g[e], a_ln_b[e]])
            xs = _conv_mixer(xs, g[0:2], a_w_in[e].astype(BF16), _rep_rows(a_conv_a[e]),
                             _rep_rows(a_conv_b[e]), vec, a_w_out[e].astype(BF16), seq, tm_mix)
        else:
            o = layer // 2
            xs = xt.reshape(MLSTM_STREAMS, t // MLSTM_STREAMS, d)
            consts = _mlstm_consts(m_w_in[o], m_i_bias[o], m_f_bias[o], m_norm_g[o], d, tm_mix)
            xs = _mlstm_mixer(xs, g[0:2], *consts, m_w_out[o].astype(BF16), seq, tm_mix)
        xt = xs.reshape(t, d)
        w_kv = x_w_kv[layer]
        kt, vm = _mem_kv(mem, mem_norm_g[layer][None, :], w_kv[:, :d].T.astype(BF16),
                         w_kv[:, d:].astype(BF16))
        xt = _xattn(xt, g[2:4], x_w_q[layer].astype(BF16), kt, vm, x_w_o[layer].astype(BF16),
                    seq, tm_att)
        xt = _ffn(xt, g[4:6], f_w_gu[layer].astype(BF16), f_w_down[layer].astype(BF16), tm_ffn)
    return xt.reshape(bsz, seq, d)
```

```python
import functools
from typing import NamedTuple

import jax
import jax.numpy as jnp
from jax import lax
from jax.experimental import pallas as pl
from jax.experimental.pallas import tpu as pltpu

F32 = jnp.float32
BF16 = jnp.bfloat16

EPS = 1e-6
CONV_A = 3
CONV_B = 31
M_HEADS = 8
X_HEADS = 4
LANES = 128
SUBLANES = 8
CONV_STREAMS = 2
MLSTM_STREAMS = 4
ATTN_PARTS = 4
FFN_PARTS = 4
CONV_ROWS = 16
MLSTM_ONES = 16
TM_CONV = 512
TM_MLSTM = 256
MLSTM_LAG = 2
TM_ATTN = 2048
TM_FFN = 1024
VMEM_LIMIT = 56 * 1024 * 1024


def _dot(a, b):
    return jnp.dot(a, b, preferred_element_type=F32)


def _dot_nt(a, b):
    return lax.dot_general(a, b, (((1,), (1,)), ((), ())), preferred_element_type=F32)


def _dot_tn(a, b):
    return lax.dot_general(a, b, (((0,), (0,)), ((), ())), preferred_element_type=F32)


def _rms(x, g):
    return x * lax.rsqrt(jnp.mean(x * x, axis=-1, keepdims=True) + EPS) * g


def _sigmoid(x):
    return 1.0 / (1.0 + jnp.exp(-x))


def _log_sigmoid(x):
    return jnp.minimum(x, 0.0) - jnp.log(1.0 + jnp.exp(-jnp.abs(x)))


def _params(n_grid):
    return pltpu.CompilerParams(dimension_semantics=("arbitrary",) * n_grid,
                                vmem_limit_bytes=VMEM_LIMIT)


class _Layer(NamedTuple):
    stack: jax.Array
    layer: int


def _resident(c):
    if isinstance(c, _Layer):
        nd = c.stack.ndim - 1
        index = (c.layer,) + (0,) * nd
        return c.stack, pl.BlockSpec((None,) + c.stack.shape[1:], lambda *_: index,
                                     pipeline_mode=pl.Buffered(1))
    index = (0,) * c.ndim
    return c, pl.BlockSpec(c.shape, lambda *_: index, pipeline_mode=pl.Buffered(1))


def _ffn_body(x_ref, g_ref, wgu_ref, wd_ref, o_ref):
    dff = wd_ref.shape[0]
    tm = x_ref.shape[0] // FFN_PARTS
    rows = [slice(p * tm, (p + 1) * tm) for p in range(FFN_PARTS)]
    hs = [_rms(x_ref[r, :], g_ref[0:1, :]).astype(BF16) for r in rows]
    acts = []
    for h in hs:
        gt = _dot(h, wgu_ref[:, 0:dff])
        up = _dot(h, wgu_ref[:, dff:2 * dff])
        acts.append((gt * _sigmoid(gt) * up).astype(BF16))
    for r, a in zip(rows, acts):
        y = _dot(a, wd_ref[...])
        o_ref[r, :] = x_ref[r, :] + _rms(y, g_ref[1:2, :])


def _ffn(x, g2, wgu, wd, tm):
    t, d = x.shape
    consts, specs = zip(*map(_resident, (g2, wgu, wd)))
    return pl.pallas_call(
        _ffn_body,
        out_shape=jax.ShapeDtypeStruct((t, d), F32),
        grid=(t // tm,),
        in_specs=[pl.BlockSpec((tm, d), lambda i: (i, 0)), *specs],
        out_specs=pl.BlockSpec((tm, d), lambda i: (i, 0)),
        compiler_params=_params(1),
        name="ffn",
    )(x, *consts)


def _kv_body(mem_ref, g_ref, wkt_ref, wv_ref, kt_ref, v_ref):
    memn = _rms(mem_ref[0], g_ref[...]).astype(BF16)
    kt_ref[0] = _dot_nt(wkt_ref[...], memn).astype(BF16)
    v_ref[0] = _dot(memn, wv_ref[...]).astype(BF16)


def _mem_kv(mem, g, wkt, wv):
    b, m, d = mem.shape
    consts, specs = zip(*map(_resident, (g, wkt, wv)))
    return pl.pallas_call(
        _kv_body,
        out_shape=(jax.ShapeDtypeStruct((b, d, m), BF16), jax.ShapeDtypeStruct((b, m, d), BF16)),
        grid=(b,),
        in_specs=[pl.BlockSpec((1, m, d), lambda i: (i, 0, 0)), *specs],
        out_specs=(pl.BlockSpec((1, d, m), lambda i: (i, 0, 0)),
                   pl.BlockSpec((1, m, d), lambda i: (i, 0, 0))),
        compiler_params=_params(1),
        name="mem_kv",
    )(mem, *consts)


def _xattn_body(x_ref, g_ref, wq_ref, kt_ref, v_ref, wo_ref, o_ref, obuf):
    tm = x_ref.shape[0] // ATTN_PARTS
    d = x_ref.shape[-1]
    dh = d // X_HEADS
    qs = [None] * ATTN_PARTS

    def rows(p):
        return slice(p * tm, (p + 1) * tm)

    def stage_q(p):
        h = _rms(x_ref[rows(p), :], g_ref[0:1, :]).astype(BF16)
        qs[p] = (_dot(h, wq_ref[...]) * (dh ** -0.5)).astype(BF16)

    def stage_att(p):
        for hd in range(X_HEADS):
            sl = slice(hd * dh, (hd + 1) * dh)
            sc = _dot(qs[p][:, sl], kt_ref[0, sl, :])
            e = jnp.exp(sc - jnp.max(sc, axis=-1, keepdims=True))
            pr = (e / jnp.sum(e, axis=-1, keepdims=True)).astype(BF16)
            obuf[rows(p), sl] = _dot(pr, v_ref[0, :, sl]).astype(BF16)

    def stage_o(p):
        y = _dot(obuf[rows(p), :], wo_ref[...])
        o_ref[rows(p), :] = x_ref[rows(p), :] + _rms(y, g_ref[1:2, :])

    stage_q(0)
    for p in range(ATTN_PARTS):
        if p + 1 < ATTN_PARTS:
            stage_q(p + 1)
        stage_att(p)
        stage_o(p)


def _xattn(x, g2, wq, kt, v, wo, seq, tm):
    t, d = x.shape
    m = v.shape[1]
    per = seq // tm
    (g2, wq, wo), (g_spec, wq_spec, wo_spec) = zip(*map(_resident, (g2, wq, wo)))
    return pl.pallas_call(
        _xattn_body,
        out_shape=jax.ShapeDtypeStruct((t, d), F32),
        grid=(t // tm,),
        in_specs=[pl.BlockSpec((tm, d), lambda i: (i, 0)), g_spec, wq_spec,
                  pl.BlockSpec((1, d, m), lambda i: (i // per, 0, 0)),
                  pl.BlockSpec((1, m, d), lambda i: (i // per, 0, 0)),
                  wo_spec],
        out_specs=pl.BlockSpec((tm, d), lambda i: (i, 0)),
        scratch_shapes=[pltpu.VMEM((tm, d), BF16)],
        compiler_params=_params(1),
        name="xattn",
    )(x, g2, wq, kt, v, wo)


def _pad_rows(taps):
    return -(-(taps - 1) // SUBLANES) * SUBLANES


def _tap_plan(taps):
    pad = _pad_rows(taps)
    plan = []
    for k in range(taps):
        off = pad - (taps - 1) + k
        plan.append((off % SUBLANES, off - off % SUBLANES))
    return plan


def _conv_body(x_ref, g_ref, win_ref, wa_ref, wb_ref, vec_ref, wout_ref, o_ref,
               abuf, bbuf, gbuf, ybuf, *, tm, da, db):
    pad_a, pad_b = _pad_rows(CONV_A), _pad_rows(CONV_B)
    plan_a, plan_b = _tap_plan(CONV_A), _tap_plan(CONV_B)
    shifts_a = sorted({s for s, _ in plan_a if s})
    shifts_b = sorted({s for s, _ in plan_b if s})
    sub = CONV_ROWS // SUBLANES
    streams = x_ref.shape[0]

    @pl.when(pl.program_id(1) == 0)
    def _():
        for st in range(streams):
            abuf[st, 0, 0:pad_a, :] = jnp.zeros((pad_a, da), F32)
            bbuf[st, 0, 0:pad_b, :] = jnp.zeros((pad_b, db), F32)

    bias, ln_g, ln_b = vec_ref[0:1, :], vec_ref[1:2, :], vec_ref[2:3, :]

    def stage_in(st):
        h = _rms(x_ref[st], g_ref[0:1, :]).astype(BF16)
        u = _dot(h, win_ref[...])
        gbuf[st] = u[:, 0:da]
        abuf[st, 0, pad_a:pad_a + tm, :] = u[:, da:2 * da] * u[:, 2 * da:3 * da]
        bbuf[st, 0, pad_b:pad_b + tm, :] = u[:, 3 * da:3 * da + db] * _sigmoid(u[:, 3 * da + db:])
        na = pad_a + tm - SUBLANES
        for i, s in enumerate(shifts_a):
            abuf[st, 1 + i, 0:na, :] = abuf[st, 0, s:s + na, :]
        nb = pad_b + tm - SUBLANES
        for i, s in enumerate(shifts_b):
            bbuf[st, 1 + i, 0:nb, :] = bbuf[st, 0, s:s + nb, :]

    def taps(w_ref, buf, st, plan, shifts, r):
        accs = [None] * sub
        for k, (s, off) in enumerate(plan):
            src = 0 if s == 0 else 1 + shifts.index(s)
            w = w_ref[k]
            for i in range(sub):
                lo = off + r + i * SUBLANES
                term = w * buf[st, src, lo:lo + SUBLANES, :]
                accs[i] = term if accs[i] is None else accs[i] + term
        return jnp.concatenate(accs, axis=0)

    def stage_conv(st):
        for r in range(0, tm, CONV_ROWS):
            ya = gbuf[st, r:r + CONV_ROWS, :] * taps(wa_ref, abuf, st, plan_a, shifts_a, r)
            ybuf[st, r:r + CONV_ROWS, 0:da] = ya.astype(BF16)
            z = taps(wb_ref, bbuf, st, plan_b, shifts_b, r) + bias
            mu = jnp.mean(z, axis=-1, keepdims=True)
            zc = z - mu
            var = jnp.mean(zc * zc, axis=-1, keepdims=True)
            zn = zc * lax.rsqrt(var + EPS) * ln_g + ln_b
            ybuf[st, r:r + CONV_ROWS, da:da + db] = (zn * _sigmoid(zn)).astype(BF16)
        abuf[st, 0, 0:pad_a, :] = abuf[st, 0, tm:tm + pad_a, :]
        bbuf[st, 0, 0:pad_b, :] = bbuf[st, 0, tm:tm + pad_b, :]

    def stage_proj(st):
        y = _dot(ybuf[st], wout_ref[...])
        o_ref[st] = x_ref[st] + _rms(y, g_ref[1:2, :])

    for st in range(streams):
        stage_in(st)
    for st in range(streams):
        stage_conv(st)
        stage_proj(st)


def _conv_mixer(x, g2, win, wa, wb, vec, wout, seq, tm):
    ns, t, d = x.shape
    da, db = wa.stack.shape[-1], wb.stack.shape[-1]
    per = seq // tm
    pad_a, pad_b = _pad_rows(CONV_A), _pad_rows(CONV_B)
    n_sh_a = 1 + len({s for s, _ in _tap_plan(CONV_A) if s})
    n_sh_b = 1 + len({s for s, _ in _tap_plan(CONV_B) if s})
    body = functools.partial(_conv_body, tm=tm, da=da, db=db)
    consts, specs = zip(*map(_resident, (g2, win, wa, wb, vec, wout)))
    return pl.pallas_call(
        body,
        out_shape=jax.ShapeDtypeStruct((ns, t, d), F32),
        grid=(t // seq, per),
        in_specs=[pl.BlockSpec((ns, tm, d), lambda b, j: (0, b * per + j, 0)), *specs],
        out_specs=pl.BlockSpec((ns, tm, d), lambda b, j: (0, b * per + j, 0)),
        scratch_shapes=[pltpu.VMEM((ns, n_sh_a, pad_a + tm, da), F32),
                        pltpu.VMEM((ns, n_sh_b, pad_b + tm, db), F32),
                        pltpu.VMEM((ns, tm, da), F32),
                        pltpu.VMEM((ns, tm, da + db), BF16)],
        compiler_params=_params(2),
        name="conv_mixer",
    )(x, *consts)


def _rep_rows(w):
    return jnp.broadcast_to(w[:, :, None, :], w.shape[:2] + (SUBLANES, w.shape[2]))


def _scan_lanes(x, op, fill):
    n = x.shape[-1]
    lane = lax.broadcasted_iota(jnp.int32, x.shape, x.ndim - 1)
    s = 1
    while s < n:
        x = op(x, jnp.where(lane >= s, pltpu.roll(x, s, x.ndim - 1), fill))
        s *= 2
    return x


def _mlstm_body(x_ref, g_ref, wt_ref, win_ref, wift_ref, bias_c_ref, negt_ref, mhg_ref, wout_ref,
                o_ref, cst, mst, hbuf, sbuf, tbuf, *, tm, d):
    dh = d // M_HEADS
    L = tm
    heads = range(M_HEADS)

    @pl.when(pl.program_id(1) == 0)
    def _():
        cst[...] = jnp.zeros(cst.shape, F32)
        mst[...] = jnp.zeros(mst.shape, F32)

    ones_rows = jnp.ones((MLSTM_ONES, L), F32)
    streams = x_ref.shape[0]
    env = [dict() for _ in range(streams)]

    def hsl(hd):
        return slice(hd * dh, (hd + 1) * dh)

    def row(a, hd):
        return a[hd:hd + 1, :]

    def stage_in(st):
        e = env[st]
        h = _rms(x_ref[st], g_ref[0:1, :]).astype(BF16)
        e["qT"] = _dot_nt(wt_ref[0:d, :], h).astype(BF16)
        e["vT"] = _dot_nt(wt_ref[d:2 * d, :], h)
        e["ogT"] = _dot_nt(wt_ref[2 * d:3 * d, :], h)
        e["k"] = (_dot(h, win_ref[:, d:2 * d]) * (dh ** -0.5)).astype(BF16)
        g_r = _dot_nt(wift_ref[...], h) + bias_c_ref[...]
        i_r = g_r[0:M_HEADS, :]
        b_r = _scan_lanes(_log_sigmoid(g_r[M_HEADS:2 * M_HEADS, :]), jnp.add, 0.0)
        r_r = i_r - b_r
        m_old = mst[st][:, 0:1]
        mm_r = jnp.maximum(m_old, _scan_lanes(r_r, jnp.maximum, -jnp.inf))
        e["mm_r"] = mm_r
        e["inter_r"] = jnp.exp(m_old - mm_r)
        e["floor_r"] = jnp.exp(-(b_r + mm_r))
        g_col = b_r[:, L - 1:L]
        a_r = g_col - b_r + i_r
        m_new = jnp.maximum(g_col + m_old, jnp.max(a_r, axis=-1, keepdims=True))
        e["wa_r"] = jnp.exp(a_r - m_new)
        e["decay"] = jnp.exp(g_col + m_old - m_new)
        mst[st] = jnp.broadcast_to(m_new, mst.shape[1:])
        r_pad = jnp.concatenate([r_r, jnp.zeros((LANES - M_HEADS, L), F32)], axis=0)
        e["r_c"] = r_pad.T

    def stage_scores(st):
        e = env[st]
        for hd in heads:
            sbuf[st, hd] = _dot(e["k"][:, hsl(hd)], e["qT"][hsl(hd), :])

    def stage_weights(st):
        e = env[st]
        e["w"] = []
        for hd in heads:
            arg = (e["r_c"][:, hd:hd + 1] + negt_ref[...]) - row(e["mm_r"], hd)
            e["w"].append((jnp.exp(arg) * sbuf[st, hd]).astype(BF16))

    def stage_tot(st):
        e = env[st]
        for hd in heads:
            vaug = jnp.concatenate([e["vT"][hsl(hd), :], ones_rows], axis=0)
            caug = cst[st, hd]
            tbuf[st, hd] = (row(e["inter_r"], hd) * _dot(caug.astype(BF16), e["qT"][hsl(hd), :])
                            + _dot(vaug.astype(BF16), e["w"][hd]))
            vw = (vaug * row(e["wa_r"], hd)).astype(BF16)
            cst[st, hd] = e["decay"][hd:hd + 1, :] * caug + _dot(vw, e["k"][:, hsl(hd)])

    def stage_out(st):
        e = env[st]
        for hd in heads:
            tot = tbuf[st, hd]
            dd = jnp.maximum(jnp.abs(tot[dh:dh + 1, :]), row(e["floor_r"], hd))
            out = tot[0:dh, :] / dd
            mu = jnp.mean(out, axis=0, keepdims=True)
            oc = out - mu
            var = jnp.mean(oc * oc, axis=0, keepdims=True)
            hn = oc * lax.rsqrt(var + EPS) * mhg_ref[hsl(hd), :]
            hbuf[st, hsl(hd), :] = (hn * _sigmoid(e["ogT"][hsl(hd), :])).astype(BF16)

    def stage_proj(st):
        y = _dot_tn(hbuf[st], wout_ref[...])
        o_ref[st] = x_ref[st] + _rms(y, g_ref[1:2, :])

    stages = (stage_in, stage_scores, stage_weights, stage_tot, stage_out, stage_proj)
    for slot in range(MLSTM_LAG * (streams - 1) + len(stages)):
        for st in reversed(range(streams)):
            k = slot - MLSTM_LAG * st
            if 0 <= k < len(stages):
                stages[k](st)


def _mlstm_mixer(x, g2, wt, win, wift, bias_c, negt, mhg, wout, seq, tm):
    ns, t, d = x.shape
    dh = d // M_HEADS
    per = seq // tm
    body = functools.partial(_mlstm_body, tm=tm, d=d)
    consts, specs = zip(*map(_resident, (g2, wt, win, wift, bias_c, negt, mhg, wout)))
    return pl.pallas_call(
        body,
        out_shape=jax.ShapeDtypeStruct((ns, t, d), F32),
        grid=(t // seq, per),
        in_specs=[pl.BlockSpec((ns, tm, d), lambda b, j: (0, b * per + j, 0)), *specs],
        out_specs=pl.BlockSpec((ns, tm, d), lambda b, j: (0, b * per + j, 0)),
        scratch_shapes=[pltpu.VMEM((ns, M_HEADS, dh + MLSTM_ONES, dh), F32),
                        pltpu.VMEM((ns, M_HEADS, LANES), F32),
                        pltpu.VMEM((ns, d, tm), BF16),
                        pltpu.VMEM((ns, M_HEADS, tm, tm), F32),
                        pltpu.VMEM((ns, M_HEADS, dh + MLSTM_ONES, tm), F32)],
        compiler_params=_params(2),
        name="mlstm_mixer",
    )(x, *consts)


def _mlstm_consts(w_in, i_bias, f_bias, mh_g, d, tm):
    w_in = w_in.astype(BF16)
    wt = jnp.swapaxes(jnp.concatenate([w_in[..., 0:d], w_in[..., 2 * d:4 * d]], axis=-1), 1, 2)
    wift = jnp.swapaxes(w_in[..., 4 * d:], 1, 2)
    bias_c = jnp.concatenate([i_bias, f_bias], axis=-1)[..., None]
    mhg = jnp.broadcast_to(mh_g[..., None], mh_g.shape + (tm,))
    idx = jnp.arange(tm)
    negt = jnp.where(idx[:, None] <= idx[None, :], 0.0, -jnp.inf).astype(F32)
    return wt, w_in, wift, bias_c, mhg, negt


def kernel(x, mem, norm_g, mem_norm_g, a_w_in, a_conv_a, a_conv_b, a_conv_b_bias, a_ln_g, a_ln_b, a_w_out, m_w_in, m_i_bias, m_f_bias, m_norm_g, m_w_out, x_w_q, x_w_kv, x_w_o, f_w_gu, f_w_down):
    bsz, seq, d = x.shape
    depth = norm_g.shape[0]
    t = bsz * seq
    tm_conv = min(TM_CONV, seq)
    tm_ml = min(TM_MLSTM, seq)
    tm_att = min(TM_ATTN, seq)
    tm_ffn = min(TM_FFN, seq)
    assert bsz % CONV_STREAMS == 0 and bsz % MLSTM_STREAMS == 0
    assert seq % tm_conv == 0 and seq % tm_ml == 0 and seq % tm_att == 0 and seq % tm_ffn == 0
    assert tm_att % ATTN_PARTS == 0 and tm_ffn % FFN_PARTS == 0
    assert d % (M_HEADS * LANES) == 0 and tm_conv % CONV_ROWS == 0

    gains = norm_g.reshape(depth * 3, 2, d)
    mem_g = mem_norm_g[:, None, :]
    a_win, a_wout = a_w_in.astype(BF16), a_w_out.astype(BF16)
    a_wa, a_wb = _rep_rows(a_conv_a), _rep_rows(a_conv_b)
    a_vec = jnp.stack([a_conv_b_bias, a_ln_g, a_ln_b], axis=1)
    m_wt, m_win, m_wift, m_bias, m_mhg, negt = _mlstm_consts(m_w_in, m_i_bias, m_f_bias, m_norm_g, d, tm_ml)
    m_wout = m_w_out.astype(BF16)
    x_wq, x_wo = x_w_q.astype(BF16), x_w_o.astype(BF16)
    x_wkt = jnp.swapaxes(x_w_kv[..., :d], 1, 2).astype(BF16)
    x_wv = x_w_kv[..., d:].astype(BF16)
    f_wgu, f_wd = f_w_gu.astype(BF16), f_w_down.astype(BF16)

    xt = x.reshape(t, d)
    for layer in range(depth):
        i = layer // 2
        if layer % 2 == 0:
            xs = xt.reshape(CONV_STREAMS, t // CONV_STREAMS, d)
            xs = _conv_mixer(xs, _Layer(gains, 3 * layer), _Layer(a_win, i), _Layer(a_wa, i),
                             _Layer(a_wb, i), _Layer(a_vec, i), _Layer(a_wout, i), seq, tm_conv)
        else:
            xs = xt.reshape(MLSTM_STREAMS, t // MLSTM_STREAMS, d)
            xs = _mlstm_mixer(xs, _Layer(gains, 3 * layer), _Layer(m_wt, i), _Layer(m_win, i),
                              _Layer(m_wift, i), _Layer(m_bias, i), negt, _Layer(m_mhg, i),
                              _Layer(m_wout, i), seq, tm_ml)
        xt = xs.reshape(t, d)
        kt, vm = _mem_kv(mem, _Layer(mem_g, layer), _Layer(x_wkt, layer), _Layer(x_wv, layer))
        xt = _xattn(xt, _Layer(gains, 3 * layer + 1), _Layer(x_wq, layer), kt, vm,
                    _Layer(x_wo, layer), seq, tm_att)
        xt = _ffn(xt, _Layer(gains, 3 * layer + 2), _Layer(f_wgu, layer), _Layer(f_wd, layer), tm_ffn)
    return xt.reshape(bsz, seq, d)
```

```python
import functools
from typing import NamedTuple

import jax
import jax.numpy as jnp
from jax import lax
from jax.experimental import pallas as pl
from jax.experimental.pallas import tpu as pltpu

F32 = jnp.float32
BF16 = jnp.bfloat16

EPS = 1e-6
CONV_A = 3
CONV_B = 31
M_HEADS = 8
X_HEADS = 4
LANES = 128
SUBLANES = 8
CONV_STREAMS = 2
MLSTM_STREAMS = 4
ATTN_PARTS = 4
FFN_PARTS = 4
CONV_ROWS = 32
MLSTM_ONES = 16
TM_CONV = 512
TM_MLSTM = 256
MLSTM_LAG = 2
TM_ATTN = 2048
TM_FFN = 1024
VMEM_LIMIT = 56 * 1024 * 1024


def _dot(a, b):
    return jnp.dot(a, b, preferred_element_type=F32)


def _dot_nt(a, b):
    return lax.dot_general(a, b, (((1,), (1,)), ((), ())), preferred_element_type=F32)


def _dot_tn(a, b):
    return lax.dot_general(a, b, (((0,), (0,)), ((), ())), preferred_element_type=F32)


def _rms(x, g):
    return x * lax.rsqrt(jnp.mean(x * x, axis=-1, keepdims=True) + EPS) * g


def _sigmoid(x):
    return 1.0 / (1.0 + jnp.exp(-x))


def _log_sigmoid(x):
    return jnp.minimum(x, 0.0) - jnp.log(1.0 + jnp.exp(-jnp.abs(x)))


def _params(n_grid):
    return pltpu.CompilerParams(dimension_semantics=("arbitrary",) * n_grid,
                                vmem_limit_bytes=VMEM_LIMIT)


class _Layer(NamedTuple):
    stack: jax.Array
    layer: int


def _resident(c):
    if isinstance(c, _Layer):
        nd = c.stack.ndim - 1
        index = (c.layer,) + (0,) * nd
        return c.stack, pl.BlockSpec((None,) + c.stack.shape[1:], lambda *_: index,
                                     pipeline_mode=pl.Buffered(1))
    index = (0,) * c.ndim
    return c, pl.BlockSpec(c.shape, lambda *_: index, pipeline_mode=pl.Buffered(1))


def _ffn_body(x_ref, g_ref, wgu_ref, wd_ref, o_ref):
    dff = wd_ref.shape[0]
    tm = x_ref.shape[0] // FFN_PARTS
    rows = [slice(p * tm, (p + 1) * tm) for p in range(FFN_PARTS)]
    hs = [_rms(x_ref[r, :], g_ref[0:1, :]).astype(BF16) for r in rows]
    acts = []
    for h in hs:
        gt = _dot(h, wgu_ref[:, 0:dff])
        up = _dot(h, wgu_ref[:, dff:2 * dff])
        acts.append((gt * _sigmoid(gt) * up).astype(BF16))
    for r, a in zip(rows, acts):
        y = _dot(a, wd_ref[...])
        o_ref[r, :] = x_ref[r, :] + _rms(y, g_ref[1:2, :])


def _ffn(x, g2, wgu, wd, tm):
    t, d = x.shape
    consts, specs = zip(*map(_resident, (g2, wgu, wd)))
    return pl.pallas_call(
        _ffn_body,
        out_shape=jax.ShapeDtypeStruct((t, d), F32),
        grid=(t // tm,),
        in_specs=[pl.BlockSpec((tm, d), lambda i: (i, 0)), *specs],
        out_specs=pl.BlockSpec((tm, d), lambda i: (i, 0)),
        compiler_params=_params(1),
        name="ffn",
    )(x, *consts)


def _kv_body(mem_ref, g_ref, wkt_ref, wv_ref, kt_ref, v_ref):
    memn = _rms(mem_ref[0], g_ref[...]).astype(BF16)
    kt_ref[0] = _dot_nt(wkt_ref[...], memn).astype(BF16)
    v_ref[0] = _dot(memn, wv_ref[...]).astype(BF16)


def _mem_kv(mem, g, wkt, wv):
    b, m, d = mem.shape
    consts, specs = zip(*map(_resident, (g, wkt, wv)))
    return pl.pallas_call(
        _kv_body,
        out_shape=(jax.ShapeDtypeStruct((b, d, m), BF16), jax.ShapeDtypeStruct((b, m, d), BF16)),
        grid=(b,),
        in_specs=[pl.BlockSpec((1, m, d), lambda i: (i, 0, 0)), *specs],
        out_specs=(pl.BlockSpec((1, d, m), lambda i: (i, 0, 0)),
                   pl.BlockSpec((1, m, d), lambda i: (i, 0, 0))),
        compiler_params=_params(1),
        name="mem_kv",
    )(mem, *consts)


def _xattn_body(x_ref, g_ref, wq_ref, kt_ref, v_ref, wo_ref, o_ref, obuf):
    tm = x_ref.shape[0] // ATTN_PARTS
    d = x_ref.shape[-1]
    dh = d // X_HEADS
    qs = [None] * ATTN_PARTS

    def rows(p):
        return slice(p * tm, (p + 1) * tm)

    def stage_q(p):
        h = _rms(x_ref[rows(p), :], g_ref[0:1, :]).astype(BF16)
        qs[p] = (_dot(h, wq_ref[...]) * (dh ** -0.5)).astype(BF16)

    def stage_att(p):
        for hd in range(X_HEADS):
            sl = slice(hd * dh, (hd + 1) * dh)
            sc = _dot(qs[p][:, sl], kt_ref[0, sl, :])
            e = jnp.exp(sc - jnp.max(sc, axis=-1, keepdims=True))
            pr = (e / jnp.sum(e, axis=-1, keepdims=True)).astype(BF16)
            obuf[rows(p), sl] = _dot(pr, v_ref[0, :, sl]).astype(BF16)

    def stage_o(p):
        y = _dot(obuf[rows(p), :], wo_ref[...])
        o_ref[rows(p), :] = x_ref[rows(p), :] + _rms(y, g_ref[1:2, :])

    stage_q(0)
    for p in range(ATTN_PARTS):
        if p + 1 < ATTN_PARTS:
            stage_q(p + 1)
        stage_att(p)
        stage_o(p)


def _xattn(x, g2, wq, kt, v, wo, seq, tm):
    t, d = x.shape
    m = v.shape[1]
    per = seq // tm
    (g2, wq, wo), (g_spec, wq_spec, wo_spec) = zip(*map(_resident, (g2, wq, wo)))
    return pl.pallas_call(
        _xattn_body,
        out_shape=jax.ShapeDtypeStruct((t, d), F32),
        grid=(t // tm,),
        in_specs=[pl.BlockSpec((tm, d), lambda i: (i, 0)), g_spec, wq_spec,
                  pl.BlockSpec((1, d, m), lambda i: (i // per, 0, 0)),
                  pl.BlockSpec((1, m, d), lambda i: (i // per, 0, 0)),
                  wo_spec],
        out_specs=pl.BlockSpec((tm, d), lambda i: (i, 0)),
        scratch_shapes=[pltpu.VMEM((tm, d), BF16)],
        compiler_params=_params(1),
        name="xattn",
    )(x, g2, wq, kt, v, wo)


def _pad_rows(taps):
    return -(-(taps - 1) // SUBLANES) * SUBLANES


def _conv_body(x_ref, g_ref, win_ref, wa_ref, wb_ref, vec_ref, wout_ref, o_ref,
               abuf, bbuf, gbuf, ybuf, *, tm, da, db):
    pad_a, pad_b = _pad_rows(CONV_A), _pad_rows(CONV_B)
    sub = CONV_ROWS // SUBLANES
    streams = x_ref.shape[0]

    @pl.when(pl.program_id(1) == 0)
    def _():
        for st in range(streams):
            abuf[st, :, 0:pad_a, :] = jnp.zeros((da // LANES, pad_a, LANES), F32)
            bbuf[st, :, 0:pad_b, :] = jnp.zeros((db // LANES, pad_b, LANES), F32)

    bias, ln_g, ln_b = vec_ref[0:1, :], vec_ref[1:2, :], vec_ref[2:3, :]

    def lanes(c):
        return slice(c * LANES, (c + 1) * LANES)

    def stage_in(st):
        h = _rms(x_ref[st], g_ref[0:1, :]).astype(BF16)
        u = _dot(h, win_ref[...])
        gbuf[st] = u[:, 0:da]
        ca = u[:, da:2 * da] * u[:, 2 * da:3 * da]
        glu = u[:, 3 * da:3 * da + db] * _sigmoid(u[:, 3 * da + db:])
        for c in range(da // LANES):
            abuf[st, c, pad_a:pad_a + tm, :] = ca[:, lanes(c)]
        for c in range(db // LANES):
            bbuf[st, c, pad_b:pad_b + tm, :] = glu[:, lanes(c)]

    def taps(w_ref, buf, st, n_taps, pad, r):
        cols = []
        for c in range(buf.shape[1]):
            accs = [None] * sub
            for k in range(n_taps):
                w = w_ref[k, :, lanes(c)]
                for i in range(sub):
                    lo = pad - (n_taps - 1) + k + r + i * SUBLANES
                    term = w * buf[st, c, lo:lo + SUBLANES, :]
                    accs[i] = term if accs[i] is None else accs[i] + term
            cols.append(jnp.concatenate(accs, axis=0))
        return jnp.concatenate(cols, axis=1)

    def stage_conv(st):
        for r in range(0, tm, CONV_ROWS):
            ya = gbuf[st, r:r + CONV_ROWS, :] * taps(wa_ref, abuf, st, CONV_A, pad_a, r)
            ybuf[st, r:r + CONV_ROWS, 0:da] = ya.astype(BF16)
            z = taps(wb_ref, bbuf, st, CONV_B, pad_b, r) + bias
            mu = jnp.mean(z, axis=-1, keepdims=True)
            zc = z - mu
            var = jnp.mean(zc * zc, axis=-1, keepdims=True)
            zn = zc * lax.rsqrt(var + EPS) * ln_g + ln_b
            ybuf[st, r:r + CONV_ROWS, da:da + db] = (zn * _sigmoid(zn)).astype(BF16)
        abuf[st, :, 0:pad_a, :] = abuf[st, :, tm:tm + pad_a, :]
        bbuf[st, :, 0:pad_b, :] = bbuf[st, :, tm:tm + pad_b, :]

    def stage_proj(st):
        y = _dot(ybuf[st], wout_ref[...])
        o_ref[st] = x_ref[st] + _rms(y, g_ref[1:2, :])

    for st in range(streams):
        stage_in(st)
    for st in range(streams):
        stage_conv(st)
        stage_proj(st)


def _conv_mixer(x, g2, win, wa, wb, vec, wout, seq, tm):
    ns, t, d = x.shape
    da, db = wa.stack.shape[-1], wb.stack.shape[-1]
    per = seq // tm
    pad_a, pad_b = _pad_rows(CONV_A), _pad_rows(CONV_B)
    assert da % LANES == 0 and db % LANES == 0
    body = functools.partial(_conv_body, tm=tm, da=da, db=db)
    consts, specs = zip(*map(_resident, (g2, win, wa, wb, vec, wout)))
    return pl.pallas_call(
        body,
        out_shape=jax.ShapeDtypeStruct((ns, t, d), F32),
        grid=(t // seq, per),
        in_specs=[pl.BlockSpec((ns, tm, d), lambda b, j: (0, b * per + j, 0)), *specs],
        out_specs=pl.BlockSpec((ns, tm, d), lambda b, j: (0, b * per + j, 0)),
        scratch_shapes=[pltpu.VMEM((ns, da // LANES, pad_a + tm, LANES), F32),
                        pltpu.VMEM((ns, db // LANES, pad_b + tm, LANES), F32),
                        pltpu.VMEM((ns, tm, da), F32),
                        pltpu.VMEM((ns, tm, da + db), BF16)],
        compiler_params=_params(2),
        name="conv_mixer",
    )(x, *consts)


def _rep_rows(w):
    return jnp.broadcast_to(w[:, :, None, :], w.shape[:2] + (SUBLANES, w.shape[2]))


def _scan_lanes(x, op, fill):
    n = x.shape[-1]
    lane = lax.broadcasted_iota(jnp.int32, x.shape, x.ndim - 1)
    s = 1
    while s < n:
        x = op(x, jnp.where(lane >= s, pltpu.roll(x, s, x.ndim - 1), fill))
        s *= 2
    return x


def _mlstm_body(x_ref, g_ref, wt_ref, win_ref, wift_ref, bias_c_ref, negt_ref, mhg_ref, wout_ref,
                o_ref, cst, mst, hbuf, sbuf, tbuf, *, tm, d):
    dh = d // M_HEADS
    L = tm
    heads = range(M_HEADS)

    @pl.when(pl.program_id(1) == 0)
    def _():
        cst[...] = jnp.zeros(cst.shape, F32)
        mst[...] = jnp.zeros(mst.shape, F32)

    ones_rows = jnp.ones((MLSTM_ONES, L), F32)
    streams = x_ref.shape[0]
    env = [dict() for _ in range(streams)]

    def hsl(hd):
        return slice(hd * dh, (hd + 1) * dh)

    def row(a, hd):
        return a[hd:hd + 1, :]

    def stage_in(st):
        e = env[st]
        h = _rms(x_ref[st], g_ref[0:1, :]).astype(BF16)
        e["qT"] = _dot_nt(wt_ref[0:d, :], h).astype(BF16)
        e["vT"] = _dot_nt(wt_ref[d:2 * d, :], h)
        e["ogT"] = _dot_nt(wt_ref[2 * d:3 * d, :], h)
        e["k"] = (_dot(h, win_ref[:, d:2 * d]) * (dh ** -0.5)).astype(BF16)
        g_r = _dot_nt(wift_ref[...], h) + bias_c_ref[...]
        i_r = g_r[0:M_HEADS, :]
        b_r = _scan_lanes(_log_sigmoid(g_r[M_HEADS:2 * M_HEADS, :]), jnp.add, 0.0)
        r_r = i_r - b_r
        m_old = mst[st][:, 0:1]
        mm_r = jnp.maximum(m_old, _scan_lanes(r_r, jnp.maximum, -jnp.inf))
        e["mm_r"] = mm_r
        e["inter_r"] = jnp.exp(m_old - mm_r)
        e["floor_r"] = jnp.exp(-(b_r + mm_r))
        g_col = b_r[:, L - 1:L]
        a_r = g_col - b_r + i_r
        m_new = jnp.maximum(g_col + m_old, jnp.max(a_r, axis=-1, keepdims=True))
        e["wa_r"] = jnp.exp(a_r - m_new)
        e["decay"] = jnp.exp(g_col + m_old - m_new)
        mst[st] = jnp.broadcast_to(m_new, mst.shape[1:])
        r_pad = jnp.concatenate([r_r, jnp.zeros((LANES - M_HEADS, L), F32)], axis=0)
        e["r_c"] = r_pad.T

    def stage_scores(st):
        e = env[st]
        for hd in heads:
            sbuf[st, hd] = _dot(e["k"][:, hsl(hd)], e["qT"][hsl(hd), :])

    def stage_weights(st):
        e = env[st]
        e["w"] = []
        for hd in heads:
            arg = (e["r_c"][:, hd:hd + 1] + negt_ref[...]) - row(e["mm_r"], hd)
            e["w"].append((jnp.exp(arg) * sbuf[st, hd]).astype(BF16))

    def stage_tot(st):
        e = env[st]
        for hd in heads:
            vaug = jnp.concatenate([e["vT"][hsl(hd), :], ones_rows], axis=0)
            caug = cst[st, hd]
            tbuf[st, hd] = (row(e["inter_r"], hd) * _dot(caug.astype(BF16), e["qT"][hsl(hd), :])
                            + _dot(vaug.astype(BF16), e["w"][hd]))
            vw = (vaug * row(e["wa_r"], hd)).astype(BF16)
            cst[st, hd] = e["decay"][hd:hd + 1, :] * caug + _dot(vw, e["k"][:, hsl(hd)])

    def stage_out(st):
        e = env[st]
        for hd in heads:
            tot = tbuf[st, hd]
            dd = jnp.maximum(jnp.abs(tot[dh:dh + 1, :]), row(e["floor_r"], hd))
            out = tot[0:dh, :] / dd
            mu = jnp.mean(out, axis=0, keepdims=True)
            oc = out - mu
            var = jnp.mean(oc * oc, axis=0, keepdims=True)
            hn = oc * lax.rsqrt(var + EPS) * mhg_ref[hsl(hd), :]
            hbuf[st, hsl(hd), :] = (hn * _sigmoid(e["ogT"][hsl(hd), :])).astype(BF16)

    def stage_proj(st):
        y = _dot_tn(hbuf[st], wout_ref[...])
        o_ref[st] = x_ref[st] + _rms(y, g_ref[1:2, :])

    stages = (stage_in, stage_scores, stage_weights, stage_tot, stage_out, stage_proj)
    for slot in range(MLSTM_LAG * (streams - 1) + len(stages)):
        for st in reversed(range(streams)):
            k = slot - MLSTM_LAG * st
            if 0 <= k < len(stages):
                stages[k](st)


def _mlstm_mixer(x, g2, wt, win, wift, bias_c, negt, mhg, wout, seq, tm):
    ns, t, d = x.shape
    dh = d // M_HEADS
    per = seq // tm
    body = functools.partial(_mlstm_body, tm=tm, d=d)
    consts, specs = zip(*map(_resident, (g2, wt, win, wift, bias_c, negt, mhg, wout)))
    return pl.pallas_call(
        body,
        out_shape=jax.ShapeDtypeStruct((ns, t, d), F32),
        grid=(t // seq, per),
        in_specs=[pl.BlockSpec((ns, tm, d), lambda b, j: (0, b * per + j, 0)), *specs],
        out_specs=pl.BlockSpec((ns, tm, d), lambda b, j: (0, b * per + j, 0)),
        scratch_shapes=[pltpu.VMEM((ns, M_HEADS, dh + MLSTM_ONES, dh), F32),
                        pltpu.VMEM((ns, M_HEADS, LANES), F32),
                        pltpu.VMEM((ns, d, tm), BF16),
                        pltpu.VMEM((ns, M_HEADS, tm, tm), F32),
                        pltpu.VMEM((ns, M_HEADS, dh + MLSTM_ONES, tm), F32)],
        compiler_params=_params(2),
        name="mlstm_mixer",
    )(x, *consts)


def _mlstm_consts(w_in, i_bias, f_bias, mh_g, d, tm):
    w_in = w_in.astype(BF16)
    wt = jnp.swapaxes(jnp.concatenate([w_in[..., 0:d], w_in[..., 2 * d:4 * d]], axis=-1), 1, 2)
    wift = jnp.swapaxes(w_in[..., 4 * d:], 1, 2)
    bias_c = jnp.concatenate([i_bias, f_bias], axis=-1)[..., None]
    mhg = jnp.broadcast_to(mh_g[..., None], mh_g.shape + (tm,))
    idx = jnp.arange(tm)
    negt = jnp.where(idx[:, None] <= idx[None, :], 0.0, -jnp.inf).astype(F32)
    return wt, w_in, wift, bias_c, mhg, negt


def kernel(x, mem, norm_g, mem_norm_g, a_w_in, a_conv_a, a_conv_b, a_conv_b_bias, a_ln_g, a_ln_b, a_w_out, m_w_in, m_i_bias, m_f_bias, m_norm_g, m_w_out, x_w_q, x_w_kv, x_w_o, f_w_gu, f_w_down):
    bsz, seq, d = x.shape
    depth = norm_g.shape[0]
    t = bsz * seq
    tm_conv = min(TM_CONV, seq)
    tm_ml = min(TM_MLSTM, seq)
    tm_att = min(TM_ATTN, seq)
    tm_ffn = min(TM_FFN, seq)
    assert bsz % CONV_STREAMS == 0 and bsz % MLSTM_STREAMS == 0
    assert seq % tm_conv == 0 and seq % tm_ml == 0 and seq % tm_att == 0 and seq % tm_ffn == 0
    assert tm_att % ATTN_PARTS == 0 and tm_ffn % FFN_PARTS == 0
    assert d % (M_HEADS * LANES) == 0 and tm_conv % CONV_ROWS == 0

    gains = norm_g.reshape(depth * 3, 2, d)
    mem_g = mem_norm_g[:, None, :]
    a_win, a_wout = a_w_in.astype(BF16), a_w_out.astype(BF16)
    a_wa, a_wb = _rep_rows(a_conv_a), _rep_rows(a_conv_b)
    a_vec = jnp.stack([a_conv_b_bias, a_ln_g, a_ln_b], axis=1)
    m_wt, m_win, m_wift, m_bias, m_mhg, negt = _mlstm_consts(m_w_in, m_i_bias, m_f_bias, m_norm_g, d, tm_ml)
    m_wout = m_w_out.astype(BF16)
    x_wq, x_wo = x_w_q.astype(BF16), x_w_o.astype(BF16)
    x_wkt = jnp.swapaxes(x_w_kv[..., :d], 1, 2).astype(BF16)
    x_wv = x_w_kv[..., d:].astype(BF16)
    f_wgu, f_wd = f_w_gu.astype(BF16), f_w_down.astype(BF16)

    xt = x.reshape(t, d)
    for layer in range(depth):
        i = layer // 2
        if layer % 2 == 0:
            xs = xt.reshape(CONV_STREAMS, t // CONV_STREAMS, d)
            xs = _conv_mixer(xs, _Layer(gains, 3 * layer), _Layer(a_win, i), _Layer(a_wa, i),
                             _Layer(a_wb, i), _Layer(a_vec, i), _Layer(a_wout, i), seq, tm_conv)
        else:
            xs = xt.reshape(MLSTM_STREAMS, t // MLSTM_STREAMS, d)
            xs = _mlstm_mixer(xs, _Layer(gains, 3 * layer), _Layer(m_wt, i), _Layer(m_win, i),
                              _Layer(m_wift, i), _Layer(m_bias, i), negt, _Layer(m_mhg, i),
                              _Layer(m_wout, i), seq, tm_ml)
        xt = xs.reshape(t, d)
        kt, vm = _mem_kv(mem, _Layer(mem_g, layer), _Layer(x_wkt, layer), _Layer(x_wv, layer))
        xt = _xattn(xt, _Layer(gains, 3 * layer + 1), _Layer(x_wq, layer), kt, vm,
                    _Layer(x_wo, layer), seq, tm_att)
        xt = _ffn(xt, _Layer(gains, 3 * layer + 2), _Layer(f_wgu, layer), _Layer(f_wd, layer), tm_ffn)
    return xt.reshape(bsz, seq, d)
```

```python
import functools
from typing import NamedTuple

import jax
import jax.numpy as jnp
from jax import lax
from jax.experimental import pallas as pl
from jax.experimental.pallas import tpu as pltpu

F32 = jnp.float32
BF16 = jnp.bfloat16

EPS = 1e-6
CONV_A = 3
CONV_B = 31
M_HEADS = 8
X_HEADS = 4
LANES = 128
SUBLANES = 8
CONV_STREAMS = 2
MLSTM_STREAMS = 4
ATTN_PARTS = 2
FFN_PARTS = 4
CONV_ROWS = 32
MLSTM_ONES = 16
TM_CONV = 512
TM_MLSTM = 256
MLSTM_LAG = 2
TM_ATTN = 2048
TM_FFN = 1024
VMEM_LIMIT = 56 * 1024 * 1024
VMEM_LIMIT_FUSED = 62 * 1024 * 1024


def _dot(a, b):
    return jnp.dot(a, b, preferred_element_type=F32)


def _dot_nt(a, b):
    return lax.dot_general(a, b, (((1,), (1,)), ((), ())), preferred_element_type=F32)


def _dot_tn(a, b):
    return lax.dot_general(a, b, (((0,), (0,)), ((), ())), preferred_element_type=F32)


def _rms(x, g):
    return x * lax.rsqrt(jnp.mean(x * x, axis=-1, keepdims=True) + EPS) * g


def _sigmoid(x):
    return 1.0 / (1.0 + jnp.exp(-x))


def _log_sigmoid(x):
    return jnp.minimum(x, 0.0) - jnp.log(1.0 + jnp.exp(-jnp.abs(x)))


def _params(n_grid):
    return pltpu.CompilerParams(dimension_semantics=("arbitrary",) * n_grid,
                                vmem_limit_bytes=VMEM_LIMIT)


class _Layer(NamedTuple):
    stack: jax.Array
    layer: int


def _resident(c):
    if isinstance(c, _Layer):
        nd = c.stack.ndim - 1
        index = (c.layer,) + (0,) * nd
        return c.stack, pl.BlockSpec((None,) + c.stack.shape[1:], lambda *_: index,
                                     pipeline_mode=pl.Buffered(1))
    index = (0,) * c.ndim
    return c, pl.BlockSpec(c.shape, lambda *_: index, pipeline_mode=pl.Buffered(1))


def _ffn_body(x_ref, g_ref, wgu_ref, wd_ref, o_ref):
    dff = wd_ref.shape[0]
    tm = x_ref.shape[0] // FFN_PARTS
    rows = [slice(p * tm, (p + 1) * tm) for p in range(FFN_PARTS)]
    hs = [_rms(x_ref[r, :], g_ref[0:1, :]).astype(BF16) for r in rows]
    acts = []
    for h in hs:
        gt = _dot(h, wgu_ref[:, 0:dff])
        up = _dot(h, wgu_ref[:, dff:2 * dff])
        acts.append((gt * _sigmoid(gt) * up).astype(BF16))
    for r, a in zip(rows, acts):
        y = _dot(a, wd_ref[...])
        o_ref[r, :] = x_ref[r, :] + _rms(y, g_ref[1:2, :])


def _ffn(x, g2, wgu, wd, tm):
    t, d = x.shape
    consts, specs = zip(*map(_resident, (g2, wgu, wd)))
    return pl.pallas_call(
        _ffn_body,
        out_shape=jax.ShapeDtypeStruct((t, d), F32),
        grid=(t // tm,),
        in_specs=[pl.BlockSpec((tm, d), lambda i: (i, 0)), *specs],
        out_specs=pl.BlockSpec((tm, d), lambda i: (i, 0)),
        compiler_params=_params(1),
        name="ffn",
    )(x, *consts)


def _kv_body(mem_ref, g_ref, wkt_ref, wv_ref, kt_ref, v_ref):
    memn = _rms(mem_ref[0], g_ref[...]).astype(BF16)
    kt_ref[0] = _dot_nt(wkt_ref[...], memn).astype(BF16)
    v_ref[0] = _dot(memn, wv_ref[...]).astype(BF16)


def _mem_kv(mem, g, wkt, wv):
    b, m, d = mem.shape
    consts, specs = zip(*map(_resident, (g, wkt, wv)))
    return pl.pallas_call(
        _kv_body,
        out_shape=(jax.ShapeDtypeStruct((b, d, m), BF16), jax.ShapeDtypeStruct((b, m, d), BF16)),
        grid=(b,),
        in_specs=[pl.BlockSpec((1, m, d), lambda i: (i, 0, 0)), *specs],
        out_specs=(pl.BlockSpec((1, d, m), lambda i: (i, 0, 0)),
                   pl.BlockSpec((1, m, d), lambda i: (i, 0, 0))),
        compiler_params=_params(1),
        name="mem_kv",
    )(mem, *consts)


def _xattn_body(x_ref, g_ref, wq_ref, kt_ref, v_ref, wo_ref, o_ref, obuf):
    tm = x_ref.shape[0] // ATTN_PARTS
    d = x_ref.shape[-1]
    dh = d // X_HEADS
    qs = [None] * ATTN_PARTS

    def rows(p):
        return slice(p * tm, (p + 1) * tm)

    def stage_q(p):
        h = _rms(x_ref[rows(p), :], g_ref[0:1, :]).astype(BF16)
        qs[p] = (_dot(h, wq_ref[...]) * (dh ** -0.5)).astype(BF16)

    def stage_att(p):
        for hd in range(X_HEADS):
            sl = slice(hd * dh, (hd + 1) * dh)
            sc = _dot(qs[p][:, sl], kt_ref[0, sl, :])
            e = jnp.exp(sc - jnp.max(sc, axis=-1, keepdims=True))
            pr = (e / jnp.sum(e, axis=-1, keepdims=True)).astype(BF16)
            obuf[rows(p), sl] = _dot(pr, v_ref[0, :, sl]).astype(BF16)

    def stage_o(p):
        y = _dot(obuf[rows(p), :], wo_ref[...])
        o_ref[rows(p), :] = x_ref[rows(p), :] + _rms(y, g_ref[1:2, :])

    stage_q(0)
    for p in range(ATTN_PARTS):
        if p + 1 < ATTN_PARTS:
            stage_q(p + 1)
        stage_att(p)
        stage_o(p)


def _xattn(x, g2, wq, kt, v, wo, seq, tm):
    t, d = x.shape
    m = v.shape[1]
    per = seq // tm
    (g2, wq, wo), (g_spec, wq_spec, wo_spec) = zip(*map(_resident, (g2, wq, wo)))
    return pl.pallas_call(
        _xattn_body,
        out_shape=jax.ShapeDtypeStruct((t, d), F32),
        grid=(t // tm,),
        in_specs=[pl.BlockSpec((tm, d), lambda i: (i, 0)), g_spec, wq_spec,
                  pl.BlockSpec((1, d, m), lambda i: (i // per, 0, 0)),
                  pl.BlockSpec((1, m, d), lambda i: (i // per, 0, 0)),
                  wo_spec],
        out_specs=pl.BlockSpec((tm, d), lambda i: (i, 0)),
        scratch_shapes=[pltpu.VMEM((tm, d), BF16)],
        compiler_params=_params(1),
        name="xattn",
    )(x, g2, wq, kt, v, wo)


def _attn_ffn_body(x_ref, ga_ref, wq_ref, kt_ref, v_ref, wo_ref, gf_ref, wgu_ref, wd_ref, o_ref,
                   obuf, xmid):
    _xattn_body(x_ref, ga_ref, wq_ref, kt_ref, v_ref, wo_ref, xmid, obuf)
    _ffn_body(xmid, gf_ref, wgu_ref, wd_ref, o_ref)


def _attn_ffn(x, ga, wq, kt, v, wo, gf, wgu, wd, seq, tm):
    t, d = x.shape
    m = v.shape[1]
    per = seq // tm
    (ga, wq, wo, gf, wgu, wd), (ga_s, wq_s, wo_s, gf_s, wgu_s, wd_s) = zip(
        *map(_resident, (ga, wq, wo, gf, wgu, wd)))
    return pl.pallas_call(
        _attn_ffn_body,
        out_shape=jax.ShapeDtypeStruct((t, d), F32),
        grid=(t // tm,),
        in_specs=[pl.BlockSpec((tm, d), lambda i: (i, 0)), ga_s, wq_s,
                  pl.BlockSpec((1, d, m), lambda i: (i // per, 0, 0)),
                  pl.BlockSpec((1, m, d), lambda i: (i // per, 0, 0)),
                  wo_s, gf_s, wgu_s, wd_s],
        out_specs=pl.BlockSpec((tm, d), lambda i: (i, 0)),
        scratch_shapes=[pltpu.VMEM((tm, d), BF16), pltpu.VMEM((tm, d), F32)],
        compiler_params=pltpu.CompilerParams(dimension_semantics=("arbitrary",),
                                             vmem_limit_bytes=VMEM_LIMIT_FUSED),
        name="attn_ffn",
    )(x, ga, wq, kt, v, wo, gf, wgu, wd)


def _pad_rows(taps):
    return -(-(taps - 1) // SUBLANES) * SUBLANES


def _conv_body(x_ref, g_ref, win_ref, wa_ref, wb_ref, vec_ref, wout_ref, o_ref,
               abuf, bbuf, gbuf, ybuf, *, tm, da, db):
    pad_a, pad_b = _pad_rows(CONV_A), _pad_rows(CONV_B)
    sub = CONV_ROWS // SUBLANES
    streams = x_ref.shape[0]

    @pl.when(pl.program_id(1) == 0)
    def _():
        for st in range(streams):
            abuf[st, :, 0:pad_a, :] = jnp.zeros((da // LANES, pad_a, LANES), F32)
            bbuf[st, :, 0:pad_b, :] = jnp.zeros((db // LANES, pad_b, LANES), F32)

    bias, ln_g, ln_b = vec_ref[0:1, :], vec_ref[1:2, :], vec_ref[2:3, :]

    def lanes(c):
        return slice(c * LANES, (c + 1) * LANES)

    def stage_in(st):
        h = _rms(x_ref[st], g_ref[0:1, :]).astype(BF16)
        u = _dot(h, win_ref[...])
        gbuf[st] = u[:, 0:da]
        ca = u[:, da:2 * da] * u[:, 2 * da:3 * da]
        glu = u[:, 3 * da:3 * da + db] * _sigmoid(u[:, 3 * da + db:])
        for c in range(da // LANES):
            abuf[st, c, pad_a:pad_a + tm, :] = ca[:, lanes(c)]
        for c in range(db // LANES):
            bbuf[st, c, pad_b:pad_b + tm, :] = glu[:, lanes(c)]

    def taps(w_ref, buf, st, n_taps, pad, r):
        cols = []
        for c in range(buf.shape[1]):
            accs = [None] * sub
            for k in range(n_taps):
                w = w_ref[k, :, lanes(c)]
                for i in range(sub):
                    lo = pad - (n_taps - 1) + k + r + i * SUBLANES
                    term = w * buf[st, c, lo:lo + SUBLANES, :]
                    accs[i] = term if accs[i] is None else accs[i] + term
            cols.append(jnp.concatenate(accs, axis=0))
        return jnp.concatenate(cols, axis=1)

    def stage_conv(st):
        for r in range(0, tm, CONV_ROWS):
            ya = gbuf[st, r:r + CONV_ROWS, :] * taps(wa_ref, abuf, st, CONV_A, pad_a, r)
            ybuf[st, r:r + CONV_ROWS, 0:da] = ya.astype(BF16)
            z = taps(wb_ref, bbuf, st, CONV_B, pad_b, r) + bias
            mu = jnp.mean(z, axis=-1, keepdims=True)
            zc = z - mu
            var = jnp.mean(zc * zc, axis=-1, keepdims=True)
            zn = zc * lax.rsqrt(var + EPS) * ln_g + ln_b
            ybuf[st, r:r + CONV_ROWS, da:da + db] = (zn * _sigmoid(zn)).astype(BF16)
        abuf[st, :, 0:pad_a, :] = abuf[st, :, tm:tm + pad_a, :]
        bbuf[st, :, 0:pad_b, :] = bbuf[st, :, tm:tm + pad_b, :]

    def stage_proj(st):
        y = _dot(ybuf[st], wout_ref[...])
        o_ref[st] = x_ref[st] + _rms(y, g_ref[1:2, :])

    for st in range(streams):
        stage_in(st)
    for st in range(streams):
        stage_conv(st)
        stage_proj(st)


def _conv_mixer(x, g2, win, wa, wb, vec, wout, seq, tm):
    ns, t, d = x.shape
    da, db = wa.stack.shape[-1], wb.stack.shape[-1]
    per = seq // tm
    pad_a, pad_b = _pad_rows(CONV_A), _pad_rows(CONV_B)
    assert da % LANES == 0 and db % LANES == 0
    body = functools.partial(_conv_body, tm=tm, da=da, db=db)
    consts, specs = zip(*map(_resident, (g2, win, wa, wb, vec, wout)))
    return pl.pallas_call(
        body,
        out_shape=jax.ShapeDtypeStruct((ns, t, d), F32),
        grid=(t // seq, per),
        in_specs=[pl.BlockSpec((ns, tm, d), lambda b, j: (0, b * per + j, 0)), *specs],
        out_specs=pl.BlockSpec((ns, tm, d), lambda b, j: (0, b * per + j, 0)),
        scratch_shapes=[pltpu.VMEM((ns, da // LANES, pad_a + tm, LANES), F32),
                        pltpu.VMEM((ns, db // LANES, pad_b + tm, LANES), F32),
                        pltpu.VMEM((ns, tm, da), F32),
                        pltpu.VMEM((ns, tm, da + db), BF16)],
        compiler_params=_params(2),
        name="conv_mixer",
    )(x, *consts)


def _rep_rows(w):
    return jnp.broadcast_to(w[:, :, None, :], w.shape[:2] + (SUBLANES, w.shape[2]))


def _scan_lanes(x, op, fill):
    n = x.shape[-1]
    lane = lax.broadcasted_iota(jnp.int32, x.shape, x.ndim - 1)
    s = 1
    while s < n:
        x = op(x, jnp.where(lane >= s, pltpu.roll(x, s, x.ndim - 1), fill))
        s *= 2
    return x


def _mlstm_body(x_ref, g_ref, wt_ref, win_ref, wift_ref, bias_c_ref, negt_ref, mhg_ref, wout_ref,
                o_ref, cst, mst, hbuf, sbuf, tbuf, *, tm, d):
    dh = d // M_HEADS
    L = tm
    heads = range(M_HEADS)

    @pl.when(pl.program_id(1) == 0)
    def _():
        cst[...] = jnp.zeros(cst.shape, F32)
        mst[...] = jnp.zeros(mst.shape, F32)

    ones_rows = jnp.ones((MLSTM_ONES, L), F32)
    streams = x_ref.shape[0]
    env = [dict() for _ in range(streams)]

    def hsl(hd):
        return slice(hd * dh, (hd + 1) * dh)

    def row(a, hd):
        return a[hd:hd + 1, :]

    def stage_in(st):
        e = env[st]
        h = _rms(x_ref[st], g_ref[0:1, :]).astype(BF16)
        e["qT"] = _dot_nt(wt_ref[0:d, :], h).astype(BF16)
        e["vT"] = _dot_nt(wt_ref[d:2 * d, :], h)
        e["ogT"] = _dot_nt(wt_ref[2 * d:3 * d, :], h)
        e["k"] = (_dot(h, win_ref[:, d:2 * d]) * (dh ** -0.5)).astype(BF16)
        g_r = _dot_nt(wift_ref[...], h) + bias_c_ref[...]
        i_r = g_r[0:M_HEADS, :]
        b_r = _scan_lanes(_log_sigmoid(g_r[M_HEADS:2 * M_HEADS, :]), jnp.add, 0.0)
        r_r = i_r - b_r
        m_old = mst[st][:, 0:1]
        mm_r = jnp.maximum(m_old, _scan_lanes(r_r, jnp.maximum, -jnp.inf))
        e["mm_r"] = mm_r
        e["inter_r"] = jnp.exp(m_old - mm_r)
        e["floor_r"] = jnp.exp(-(b_r + mm_r))
        g_col = b_r[:, L - 1:L]
        a_r = g_col - b_r + i_r
        m_new = jnp.maximum(g_col + m_old, jnp.max(a_r, axis=-1, keepdims=True))
        e["wa_r"] = jnp.exp(a_r - m_new)
        e["decay"] = jnp.exp(g_col + m_old - m_new)
        mst[st] = jnp.broadcast_to(m_new, mst.shape[1:])
        r_pad = jnp.concatenate([r_r, jnp.zeros((LANES - M_HEADS, L), F32)], axis=0)
        e["r_c"] = r_pad.T

    def stage_scores(st):
        e = env[st]
        for hd in heads:
            sbuf[st, hd] = _dot(e["k"][:, hsl(hd)], e["qT"][hsl(hd), :])

    def stage_weights(st):
        e = env[st]
        e["w"] = []
        for hd in heads:
            arg = (e["r_c"][:, hd:hd + 1] + negt_ref[...]) - row(e["mm_r"], hd)
            e["w"].append((jnp.exp(arg) * sbuf[st, hd]).astype(BF16))

    def stage_tot(st):
        e = env[st]
        for hd in heads:
            vaug = jnp.concatenate([e["vT"][hsl(hd), :], ones_rows], axis=0)
            caug = cst[st, hd]
            tbuf[st, hd] = (row(e["inter_r"], hd) * _dot(caug.astype(BF16), e["qT"][hsl(hd), :])
                            + _dot(vaug.astype(BF16), e["w"][hd]))
            vw = (vaug * row(e["wa_r"], hd)).astype(BF16)
            cst[st, hd] = e["decay"][hd:hd + 1, :] * caug + _dot(vw, e["k"][:, hsl(hd)])

    def stage_out(st):
        e = env[st]
        for hd in heads:
            tot = tbuf[st, hd]
            dd = jnp.maximum(jnp.abs(tot[dh:dh + 1, :]), row(e["floor_r"], hd))
            out = tot[0:dh, :] / dd
            mu = jnp.mean(out, axis=0, keepdims=True)
            oc = out - mu
            var = jnp.mean(oc * oc, axis=0, keepdims=True)
            hn = oc * lax.rsqrt(var + EPS) * mhg_ref[hsl(hd), :]
            hbuf[st, hsl(hd), :] = (hn * _sigmoid(e["ogT"][hsl(hd), :])).astype(BF16)

    def stage_proj(st):
        y = _dot_tn(hbuf[st], wout_ref[...])
        o_ref[st] = x_ref[st] + _rms(y, g_ref[1:2, :])

    stages = (stage_in, stage_scores, stage_weights, stage_tot, stage_out, stage_proj)
    for slot in range(MLSTM_LAG * (streams - 1) + len(stages)):
        for st in reversed(range(streams)):
            k = slot - MLSTM_LAG * st
            if 0 <= k < len(stages):
                stages[k](st)


def _mlstm_mixer(x, g2, wt, win, wift, bias_c, negt, mhg, wout, seq, tm):
    ns, t, d = x.shape
    dh = d // M_HEADS
    per = seq // tm
    body = functools.partial(_mlstm_body, tm=tm, d=d)
    consts, specs = zip(*map(_resident, (g2, wt, win, wift, bias_c, negt, mhg, wout)))
    return pl.pallas_call(
        body,
        out_shape=jax.ShapeDtypeStruct((ns, t, d), F32),
        grid=(t // seq, per),
        in_specs=[pl.BlockSpec((ns, tm, d), lambda b, j: (0, b * per + j, 0)), *specs],
        out_specs=pl.BlockSpec((ns, tm, d), lambda b, j: (0, b * per + j, 0)),
        scratch_shapes=[pltpu.VMEM((ns, M_HEADS, dh + MLSTM_ONES, dh), F32),
                        pltpu.VMEM((ns, M_HEADS, LANES), F32),
                        pltpu.VMEM((ns, d, tm), BF16),
                        pltpu.VMEM((ns, M_HEADS, tm, tm), F32),
                        pltpu.VMEM((ns, M_HEADS, dh + MLSTM_ONES, tm), F32)],
        compiler_params=_params(2),
        name="mlstm_mixer",
    )(x, *consts)


def _mlstm_consts(w_in, i_bias, f_bias, mh_g, d, tm):
    w_in = w_in.astype(BF16)
    wt = jnp.swapaxes(jnp.concatenate([w_in[..., 0:d], w_in[..., 2 * d:4 * d]], axis=-1), 1, 2)
    wift = jnp.swapaxes(w_in[..., 4 * d:], 1, 2)
    bias_c = jnp.concatenate([i_bias, f_bias], axis=-1)[..., None]
    mhg = jnp.broadcast_to(mh_g[..., None], mh_g.shape + (tm,))
    idx = jnp.arange(tm)
    negt = jnp.where(idx[:, None] <= idx[None, :], 0.0, -jnp.inf).astype(F32)
    return wt, w_in, wift, bias_c, mhg, negt


def kernel(x, mem, norm_g, mem_norm_g, a_w_in, a_conv_a, a_conv_b, a_conv_b_bias, a_ln_g, a_ln_b, a_w_out, m_w_in, m_i_bias, m_f_bias, m_norm_g, m_w_out, x_w_q, x_w_kv, x_w_o, f_w_gu, f_w_down):
    bsz, seq, d = x.shape
    depth = norm_g.shape[0]
    t = bsz * seq
    tm_conv = min(TM_CONV, seq)
    tm_ml = min(TM_MLSTM, seq)
    tm_att = min(TM_ATTN, seq)
    tm_ffn = min(TM_FFN, seq)
    assert bsz % CONV_STREAMS == 0 and bsz % MLSTM_STREAMS == 0
    assert seq % tm_conv == 0 and seq % tm_ml == 0 and seq % tm_att == 0 and seq % tm_ffn == 0
    assert tm_att % ATTN_PARTS == 0 and tm_ffn % FFN_PARTS == 0
    assert d % (M_HEADS * LANES) == 0 and tm_conv % CONV_ROWS == 0

    gains = norm_g.reshape(depth * 3, 2, d)
    mem_g = mem_norm_g[:, None, :]
    a_win, a_wout = a_w_in.astype(BF16), a_w_out.astype(BF16)
    a_wa, a_wb = _rep_rows(a_conv_a), _rep_rows(a_conv_b)
    a_vec = jnp.stack([a_conv_b_bias, a_ln_g, a_ln_b], axis=1)
    m_wt, m_win, m_wift, m_bias, m_mhg, negt = _mlstm_consts(m_w_in, m_i_bias, m_f_bias, m_norm_g, d, tm_ml)
    m_wout = m_w_out.astype(BF16)
    x_wq, x_wo = x_w_q.astype(BF16), x_w_o.astype(BF16)
    x_wkt = jnp.swapaxes(x_w_kv[..., :d], 1, 2).astype(BF16)
    x_wv = x_w_kv[..., d:].astype(BF16)
    f_wgu, f_wd = f_w_gu.astype(BF16), f_w_down.astype(BF16)

    xt = x.reshape(t, d)
    for layer in range(depth):
        i = layer // 2
        if layer % 2 == 0:
            xs = xt.reshape(CONV_STREAMS, t // CONV_STREAMS, d)
            xs = _conv_mixer(xs, _Layer(gains, 3 * layer), _Layer(a_win, i), _Layer(a_wa, i),
                             _Layer(a_wb, i), _Layer(a_vec, i), _Layer(a_wout, i), seq, tm_conv)
        else:
            xs = xt.reshape(MLSTM_STREAMS, t // MLSTM_STREAMS, d)
            xs = _mlstm_mixer(xs, _Layer(gains, 3 * layer), _Layer(m_wt, i), _Layer(m_win, i),
                              _Layer(m_wift, i), _Layer(m_bias, i), negt, _Layer(m_mhg, i),
                              _Layer(m_wout, i), seq, tm_ml)
        xt = xs.reshape(t, d)
        kt, vm = _mem_kv(mem, _Layer(mem_g, layer), _Layer(x_wkt, layer), _Layer(x_wv, layer))
        xt = _attn_ffn(xt, _Layer(gains, 3 * layer + 1), _Layer(x_wq, layer), kt, vm, _Layer(x_wo, layer),
                       _Layer(gains, 3 * layer + 2), _Layer(f_wgu, layer), _Layer(f_wd, layer), seq, tm_ffn)
    return xt.reshape(bsz, seq, d)
```
